```python
import jax, jax.numpy as jnp
from jax import lax

D_MODEL = 1024
BATCH = 4
SEQ = 4096
DEPTH = 2

D_RNN = 1024
RNN_BLOCKS = 16
RNN_BLOCK = D_RNN // RNN_BLOCKS
CONV_A = 4
LRU_C = 8.0
HEAD_DIM = 64
ATTN_GROUPS = ((128, 1), (512, 4), (2048, 16))
N_GROUPS = len(ATTN_GROUPS)
HEADS_PER_GROUP = 4
N_ATTN_HEADS = N_GROUPS * HEADS_PER_GROUP
D_ATTN = N_ATTN_HEADS * HEAD_DIM
D_ATTN_OUT = HEADS_PER_GROUP * HEAD_DIM
ROT_DIM = HEAD_DIM // 4
ROPE_THETA = 500000.0
Q_BLOCK = 128
RWKV_HEAD = 64
D_RWKV = 1024
N_RWKV_HEADS = D_RWKV // RWKV_HEAD
DECAY_LORA = 64
AAA_LORA = 64
GATE_LORA = 128
MV_LORA = 32
RWKV_GN_EPS = 64e-5
N_SHIFT = 3 * D_RWKV + DECAY_LORA + AAA_LORA + GATE_LORA
N_BRANCH = 3
IN_SPLITS = (D_RNN, D_RNN, D_ATTN, D_ATTN, D_ATTN, N_SHIFT, N_BRANCH * D_MODEL)
N_IN = sum(IN_SPLITS)
RWKV_SPLITS = (D_RWKV, D_RWKV, D_RWKV, DECAY_LORA, AAA_LORA, GATE_LORA)
D_FF = 2816
CONV_F = 3
ALPHA = (2 * DEPTH) ** 0.25
BETA = (8 * DEPTH) ** -0.25
LN_EPS = 1e-5

kernel_name = 'hybrid_rglru_dilattn_rwkv7_deepnorm'


def _split(z, sizes):
    idx, acc = [], 0
    for s in sizes[:-1]:
        acc += s
        idx.append(acc)
    return jnp.split(z, idx, axis=-1)


def _layer_norm(x, w, b):
    xf = x.astype(jnp.float32)
    mu = xf.mean(-1, keepdims=True)
    var = jnp.square(xf - mu).mean(-1, keepdims=True)
    return ((xf - mu) * lax.rsqrt(var + LN_EPS) * w + b).astype(x.dtype)


def _causal_dwconv(u, w, b):
    k_w = w.shape[0]
    s = u.shape[1]
    up = jnp.pad(u, ((0, 0), (k_w - 1, 0), (0, 0)))
    return b + sum(w[j] * up[:, k_w - 1 - j:k_w - 1 - j + s] for j in range(k_w))


def _token_shift(z, mu):
    z_prev = jnp.pad(z, ((0, 0), (1, 0), (0, 0)))[:, :-1]
    return z + (z_prev - z) * mu


def _partial_rope(t, positions):
    half = ROT_DIM // 2
    inv_freq = ROPE_THETA ** (-jnp.arange(half, dtype=jnp.float32) / half)
    ang = positions.astype(jnp.float32)[..., None] * inv_freq
    cos = jnp.cos(ang)[:, :, None, :]
    sin = jnp.sin(ang)[:, :, None, :]
    tf = t.astype(jnp.float32)
    x1, x2, rest = tf[..., :half], tf[..., half:ROT_DIM], tf[..., ROT_DIM:]
    out = jnp.concatenate([x1 * cos - x2 * sin, x2 * cos + x1 * sin, rest], axis=-1)
    return out.astype(t.dtype)


def _rg_lru(xa, wa, ba, wx, bx, lam):
    bsz, s, _ = xa.shape
    xf = xa.astype(jnp.float32)
    xb = xf.reshape(bsz, s, RNN_BLOCKS, RNN_BLOCK)
    r = jax.nn.sigmoid(jnp.einsum('bsgi,gij->bsgj', xb, wa).reshape(bsz, s, D_RNN) + ba)
    i = jax.nn.sigmoid(jnp.einsum('bsgi,gij->bsgj', xb, wx).reshape(bsz, s, D_RNN) + bx)
    log_a = -LRU_C * r * jax.nn.softplus(-lam)
    a = jnp.exp(log_a)
    b = jnp.sqrt(-jnp.expm1(2.0 * log_a)) * (i * xf)

    def combine(left, right):
        a1, b1 = left
        a2, b2 = right
        return a1 * a2, a2 * b1 + b2

    _, h = lax.associative_scan(combine, (a, b), axis=1)
    return h.astype(xa.dtype)


def _dilated_attention(q, k, v):
    bsz, s = q.shape[:2]
    n_blocks = s // Q_BLOCK
    scale = HEAD_DIM ** -0.5
    qg = q.reshape(bsz, s, N_GROUPS, HEADS_PER_GROUP, HEAD_DIM)
    kg = k.reshape(bsz, s, N_GROUPS, HEADS_PER_GROUP, HEAD_DIM)
    vg = v.reshape(bsz, s, N_GROUPS, HEADS_PER_GROUP, HEAD_DIM)
    k_groups = [kg[:, :, g] for g in range(N_GROUPS)]
    v_groups = [vg[:, :, g] for g in range(N_GROUPS)]

    def block(q0):
        t = q0 + jnp.arange(Q_BLOCK)
        qb = lax.dynamic_slice_in_dim(qg, q0, Q_BLOCK, axis=1).astype(jnp.float32) * scale
        outs, lses = [], []
        for g, (window, dil) in enumerate(ATTN_GROUPS):
            n_keys = window // dil + 1
            idx = t[:, None] - dil * jnp.arange(n_keys)[None, :]
            valid = idx >= 0
            idx = jnp.maximum(idx, 0)
            kb = jnp.take(k_groups[g], idx, axis=1).astype(jnp.float32)
            vb = jnp.take(v_groups[g], idx, axis=1).astype(jnp.float32)
            sc = jnp.einsum('bqhd,bqjhd->bhqj', qb[:, :, g], kb)
            sc = jnp.where(valid[None, None], sc, -jnp.inf)
            lse = jax.nn.logsumexp(sc, axis=-1)
            p = jnp.exp(sc - lse[..., None])
            outs.append(jnp.einsum('bhqj,bqjhd->bqhd', p, vb))
            lses.append(lse)
        wts = jax.nn.softmax(jnp.stack(lses, 0), axis=0)
        return jnp.einsum('gbhq,gbqhd->bqhd', wts, jnp.stack(outs, 0))

    out = lax.map(block, jnp.arange(n_blocks) * Q_BLOCK)
    out = out.transpose(1, 0, 2, 3, 4).reshape(bsz, s, D_ATTN_OUT)
    return out.astype(q.dtype)


def _wkv7(r, decay, k, v, kk, a):
    bsz, _, nh, n = r.shape

    def step(st, inp):
        r_t, w_t, k_t, v_t, kk_t, a_t = inp
        sa = jnp.einsum('bhij,bhj->bhi', st, -kk_t)
        st = (st * w_t[:, :, None, :] + sa[..., None] * (kk_t * a_t)[:, :, None, :]
              + v_t[..., None] * k_t[:, :, None, :])
        return st, jnp.einsum('bhij,bhj->bhi', st, r_t)

    xs = tuple(jnp.moveaxis(t, 1, 0) for t in (r, decay, k, v, kk, a))
    _, y = lax.scan(step, jnp.zeros((bsz, nh, n, n), jnp.float32), xs)
    return jnp.moveaxis(y, 0, 1)


def _rwkv7_branch(zc, v_res, w0, w2, a0, a2, g2, k_k, k_a, r_k, ln_w, ln_b):
    bsz, s, _ = zc.shape
    r, k, v, zw, za, zg = _split(zc.astype(jnp.float32), RWKV_SPLITS)
    w_log = -jax.nn.softplus(-(w0 + jnp.tanh(zw) @ w2)) - 0.5
    decay = jnp.exp(-jnp.exp(w_log))
    a = jax.nn.sigmoid(a0 + za @ a2)
    g = jax.nn.sigmoid(zg) @ g2
    if v_res is None:
        v_first = v
    else:
        v_first, zv1, v0, v2 = v_res
        v = v + (v_first - v) * jax.nn.sigmoid(v0 + zv1.astype(jnp.float32) @ v2)

    def heads(t):
        return t.reshape(bsz, s, N_RWKV_HEADS, RWKV_HEAD)

    kk = heads(k * k_k)
    kk = kk * lax.rsqrt(jnp.sum(kk * kk, -1, keepdims=True) + 1e-12)
    k = k * (1.0 + (a - 1.0) * k_a)
    rh, kh, vh = heads(r), heads(k), heads(v)
    y = _wkv7(rh, heads(decay), kh, vh, kk, heads(a))
    mu = y.mean(-1, keepdims=True)
    var = jnp.square(y - mu).mean(-1, keepdims=True)
    y = ((y - mu) * lax.rsqrt(var + RWKV_GN_EPS)).reshape(bsz, s, D_RWKV) * ln_w + ln_b
    bonus = (jnp.sum(rh * kh * r_k, -1, keepdims=True) * vh).reshape(bsz, s, D_RWKV)
    return ((y + bonus) * g).astype(zc.dtype), v_first


def setup_inputs(seed: int = 0) -> dict:
    key = jax.random.key(seed)
    keys = iter(jax.random.split(key, 64))

    def nrm(shape, scale):
        return jax.random.normal(next(keys), shape, jnp.float32) * scale

    def uni(shape, lo, hi):
        return jax.random.uniform(next(keys), shape, jnp.float32, lo, hi)

    L, LV = DEPTH, DEPTH - 1
    x = nrm((BATCH, SEQ, D_MODEL), 1.0)
    c = nrm((BATCH, D_MODEL), 1.0)
    offset = jax.random.randint(next(keys), (BATCH, 1), 0, 1024, jnp.int32)
    positions = offset + jnp.arange(SEQ, dtype=jnp.int32)[None, :]
    u = uni((L, D_RNN), 0.9, 0.999)
    s_lam = u ** (1.0 / LRU_C)
    lru_lambda = jnp.log(s_lam) - jnp.log1p(-s_lam)
    return {
        'x': x, 'c': c, 'positions': positions,
        'mod_w': nrm((L, D_MODEL, 6 * D_MODEL), 0.5 * D_MODEL ** -0.5),
        'mod_b': nrm((L, 6 * D_MODEL), 0.01),
        'w_in': nrm((L, D_MODEL, N_IN), D_MODEL ** -0.5),
        'w_in_vres': nrm((LV, D_MODEL, MV_LORA), D_MODEL ** -0.5),
        'conv_a_w': nrm((L, CONV_A, D_RNN), CONV_A ** -0.5),
        'conv_a_b': nrm((L, D_RNN), 0.01),
        'lru_wa': nrm((L, RNN_BLOCKS, RNN_BLOCK, RNN_BLOCK), RNN_BLOCK ** -0.5),
        'lru_ba': nrm((L, D_RNN), 0.01),
        'lru_wx': nrm((L, RNN_BLOCKS, RNN_BLOCK, RNN_BLOCK), RNN_BLOCK ** -0.5),
        'lru_bx': nrm((L, D_RNN), 0.01),
        'lru_lambda': lru_lambda,
        'rwkv_mu': uni((L, N_SHIFT), 0.0, 1.0),
        'mu_vres': uni((LV, MV_LORA), 0.0, 1.0),
        'w0': uni((L, D_RWKV), -4.0, 1.0),
        'w2': nrm((L, DECAY_LORA, D_RWKV), 0.1 * DECAY_LORA ** -0.5),
        'a0': nrm((L, D_RWKV), 0.1),
        'a2': nrm((L, AAA_LORA, D_RWKV), 0.1 * AAA_LORA ** -0.5),
        'g2': nrm((L, GATE_LORA, D_RWKV), GATE_LORA ** -0.5),
        'v0': nrm((LV, D_RWKV), 0.1),
        'v2': nrm((LV, MV_LORA, D_RWKV), 0.1 * MV_LORA ** -0.5),
        'k_k': uni((L, D_RWKV), 0.7, 1.0),
        'k_a': uni((L, D_RWKV), 0.8, 1.2),
        'r_k': nrm((L, N_RWKV_HEADS, RWKV_HEAD), 0.1),
        'ln_x_w': 1.0 + nrm((L, D_RWKV), 0.02),
        'ln_x_b': nrm((L, D_RWKV), 0.01),
        'proj_a': nrm((L, D_RNN, D_MODEL), BETA * D_RNN ** -0.5),
        'proj_b': nrm((L, D_ATTN_OUT, D_MODEL), BETA * D_ATTN_OUT ** -0.5),
        'proj_c': nrm((L, D_RWKV, D_MODEL), BETA * D_RWKV ** -0.5),
        'w_o': nrm((L, D_MODEL, D_MODEL), BETA * D_MODEL ** -0.5),
        'ln1_w': 1.0 + nrm((L, D_MODEL), 0.02),
        'ln1_b': nrm((L, D_MODEL), 0.01),
        'ffn_up': nrm((L, D_MODEL, 2 * D_FF), D_MODEL ** -0.5),
        'ffn_conv_w': nrm((L, CONV_F, 2 * D_FF), CONV_F ** -0.5),
        'ffn_conv_b': nrm((L, 2 * D_FF), 0.01),
        'ffn_down': nrm((L, D_FF, D_MODEL), BETA * D_FF ** -0.5),
        'ln2_w': 1.0 + nrm((L, D_MODEL), 0.02),
        'ln2_b': nrm((L, D_MODEL), 0.01),
    }


def reference(x, c, positions, mod_w, mod_b, w_in, w_in_vres, conv_a_w, conv_a_b,
              lru_wa, lru_ba, lru_wx, lru_bx, lru_lambda, rwkv_mu, mu_vres, w0, w2,
              a0, a2, g2, v0, v2, k_k, k_a, r_k, ln_x_w, ln_x_b, proj_a, proj_b,
              proj_c, w_o, ln1_w, ln1_b, ffn_up, ffn_conv_w, ffn_conv_b, ffn_down,
              ln2_w, ln2_b):
    bsz, s = x.shape[:2]
    v_first = None
    for l in range(DEPTH):
        mod = jax.nn.silu(c) @ mod_w[l] + mod_b[l]
        sh1, sc1, gt1, sh2, sc2, gt2 = jnp.split(mod[:, None, :], 6, axis=-1)

        h = x * (1.0 + sc1) + sh1
        if l == 0:
            z = h @ w_in[l]
            xa, ga, q, k, v, zc, zgate = _split(z, IN_SPLITS)
            v_res = None
        else:
            z = h @ jnp.concatenate([w_in[l], w_in_vres[l - 1]], axis=1)
            xa, ga, q, k, v, zc, zgate, zv1 = _split(z, IN_SPLITS + (MV_LORA,))
            v_res = (v_first, _token_shift(zv1, mu_vres[l - 1]), v0[l - 1], v2[l - 1])

        xa = _causal_dwconv(xa, conv_a_w[l], conv_a_b[l])
        y_a = _rg_lru(xa, lru_wa[l], lru_ba[l], lru_wx[l], lru_bx[l], lru_lambda[l]) * jax.nn.gelu(ga)

        q = _partial_rope(q.reshape(bsz, s, N_ATTN_HEADS, HEAD_DIM), positions)
        k = _partial_rope(k.reshape(bsz, s, N_ATTN_HEADS, HEAD_DIM), positions)
        y_b = _dilated_attention(q, k, v.reshape(bsz, s, N_ATTN_HEADS, HEAD_DIM))

        y_c, v_first = _rwkv7_branch(_token_shift(zc, rwkv_mu[l]), v_res, w0[l], w2[l], a0[l],
                                     a2[l], g2[l], k_k[l], k_a[l], r_k[l], ln_x_w[l], ln_x_b[l])

        g_a, g_b, g_c = jnp.split(jax.nn.sigmoid(zgate), N_BRANCH, axis=-1)
        merged = g_a * (y_a @ proj_a[l]) + g_b * (y_b @ proj_b[l]) + g_c * (y_c @ proj_c[l])
        x = _layer_norm(ALPHA * x + (1.0 + gt1) * (merged @ w_o[l]), ln1_w[l], ln1_b[l])

        h = x * (1.0 + sc2) + sh2
        u = _causal_dwconv(h @ ffn_up[l], ffn_conv_w[l], ffn_conv_b[l])
        u_g, u_v = jnp.split(u, 2, axis=-1)
        y = (jax.nn.silu(u_g) * u_v) @ ffn_down[l]
        x = _layer_norm(ALPHA * x + (1.0 + gt2) * y, ln2_w[l], ln2_b[l])
    return x
```

```python
import functools

import jax
import jax.numpy as jnp
from jax import lax
from jax.experimental import pallas as pl
from jax.experimental.pallas import tpu as pltpu

F32 = jnp.float32
BF16 = jnp.bfloat16

D_MODEL = 1024
DEPTH = 2
D_RNN = 1024
RNN_BLOCK = 64
CONV_A = 4
LRU_C = 8.0
HEAD_DIM = 64
ATTN_GROUPS = ((128, 1), (512, 4), (2048, 16))
HEADS_PER_GROUP = 4
N_GROUPS = 3
D_ATTN = 768
D_ATTN_OUT = 256
ROT_DIM = 16
ROPE_THETA = 500000.0
D_RWKV = 1024
LORA_W = 64
LORA_A = 64
LORA_G = 128
MV_LORA = 32
RWKV_GN_EPS = 64e-5
D_FF = 2816
CONV_F = 3
ALPHA = (2 * DEPTH) ** 0.25
LN_EPS = 1e-5

LANES = 128
SUBLANES = 8
BF16_ROWS = 16

OFF_XA = 0
OFF_GA = 1024
OFF_R = 2048
OFF_K = 3072
OFF_V = 4096
OFF_GATE = 5120
OFF_QKV = 8192
OFF_LORA = 10496
N_Z = 10752

WKV_CHUNK = 64
ATTN_BLOCK = 128
VMEM_LIMIT = 52 * 1024 * 1024


def _cparams(sem):
    return pltpu.CompilerParams(dimension_semantics=sem, vmem_limit_bytes=VMEM_LIMIT)


def _dot(a, b):
    return jnp.dot(a.astype(BF16), b.astype(BF16), preferred_element_type=F32)


def _dot_nt(a, b):
    return lax.dot_general(a.astype(BF16), b.astype(BF16), (((1,), (1,)), ((), ())),
                           preferred_element_type=F32)


def _dot_tn(a, b):
    return lax.dot_general(a.astype(BF16), b.astype(BF16), (((0,), (0,)), ((), ())),
                           preferred_element_type=F32)


def _split3(x):
    p0 = x.astype(BF16)
    r1 = x - p0.astype(F32)
    p1 = r1.astype(BF16)
    p2 = (r1 - p1.astype(F32)).astype(BF16)
    return p0, p1, p2


def _head_ones():
    i = lax.broadcasted_iota(jnp.int32, (LANES, LANES), 0) // HEAD_DIM
    j = lax.broadcasted_iota(jnp.int32, (LANES, LANES), 1) // HEAD_DIM
    return jnp.where(i == j, 1.0, 0.0).astype(BF16)


def _head_sum(x, ones):
    p0, p1, p2 = _split3(x)
    outs = []
    for c in range(x.shape[1] // LANES):
        sl = slice(c * LANES, (c + 1) * LANES)
        outs.append(jnp.dot(p0[:, sl], ones, preferred_element_type=F32)
                    + jnp.dot(p1[:, sl], ones, preferred_element_type=F32)
                    + jnp.dot(p2[:, sl], ones, preferred_element_type=F32))
    return outs[0] if len(outs) == 1 else jnp.concatenate(outs, axis=1)


def _softplus(x):
    return jnp.maximum(x, 0.0) + jnp.log1p(jnp.exp(-jnp.abs(x)))


def _layer_norm(x, w, b):
    mu = jnp.mean(x, axis=-1, keepdims=True)
    d = x - mu
    var = jnp.mean(d * d, axis=-1, keepdims=True)
    return d * lax.rsqrt(var + LN_EPS) * w + b


def _mod_kernel(c_ref, w_ref, b_ref, o_ref):
    c = c_ref[...]
    s = c * jax.nn.sigmoid(c)
    s0, s1, s2 = _split3(s)
    w0, w1, w2 = _split3(w_ref[...])
    acc = jnp.dot(s0, w0, preferred_element_type=F32)
    acc += jnp.dot(s0, w1, preferred_element_type=F32) + jnp.dot(s1, w0, preferred_element_type=F32)
    acc += (jnp.dot(s1, w1, preferred_element_type=F32) + jnp.dot(s0, w2, preferred_element_type=F32)
            + jnp.dot(s2, w0, preferred_element_type=F32))
    o_ref[...] = acc + b_ref[...]


def _modulation(c, mod_w, mod_b):
    depth, d, n = mod_w.shape
    bsz = c.shape[0]
    rows = BF16_ROWS
    tn = 1536
    c_pad = jnp.pad(c, ((0, rows - bsz), (0, 0)))
    out = pl.pallas_call(
        _mod_kernel,
        grid=(depth, n // tn),
        in_specs=[pl.BlockSpec((rows, d), lambda l, j: (0, 0)),
                  pl.BlockSpec((None, d, tn), lambda l, j: (l, 0, j)),
                  pl.BlockSpec((None, 1, tn), lambda l, j: (l, 0, j))],
        out_specs=pl.BlockSpec((None, rows, tn), lambda l, j: (l, 0, j)),
        out_shape=jax.ShapeDtypeStruct((depth, rows, n), F32),
        compiler_params=_cparams(("arbitrary", "arbitrary")),
        name="adaln_mod",
    )(c_pad, mod_w, mod_b.reshape(depth, 1, n))
    return out[:, :bsz]


def _inproj_kernel(x_ref, sc_ref, sh_ref, w_ref, *rest, has_vres):
    if has_vres:
        wv_ref, z_ref, zv_ref, h_scr = rest
    else:
        z_ref, h_scr = rest

    @pl.when(pl.program_id(2) == 0)
    def _():
        h_scr[...] = (x_ref[...] * (1.0 + sc_ref[...]) + sh_ref[...]).astype(BF16)
        if has_vres:
            zv_ref[...] = jnp.dot(h_scr[...], wv_ref[...], preferred_element_type=F32)

    z_ref[...] = jnp.dot(h_scr[...], w_ref[...], preferred_element_type=F32)


def _in_projection(x, sc, sh, w, w_vres):
    bsz, s, d = x.shape
    n = w.shape[1]
    tm, tn = 1024, 1536
    has_vres = w_vres is not None
    in_specs = [pl.BlockSpec((None, tm, d), lambda b, i, j: (b, i, 0)),
                pl.BlockSpec((None, 1, d), lambda b, i, j: (b, 0, 0)),
                pl.BlockSpec((None, 1, d), lambda b, i, j: (b, 0, 0)),
                pl.BlockSpec((d, tn), lambda b, i, j: (0, j))]
    out_specs = [pl.BlockSpec((None, tm, tn), lambda b, i, j: (b, i, j))]
    out_shape = [jax.ShapeDtypeStruct((bsz, s, n), F32)]
    args = [x, sc, sh, w]
    if has_vres:
        in_specs.append(pl.BlockSpec((d, LANES), lambda b, i, j: (0, 0)))
        out_specs.append(pl.BlockSpec((None, tm, LANES), lambda b, i, j: (b, i, 0)))
        out_shape.append(jax.ShapeDtypeStruct((bsz, s, LANES), F32))
        args.append(w_vres)
    outs = pl.pallas_call(
        functools.partial(_inproj_kernel, has_vres=has_vres),
        grid=(bsz, s // tm, n // tn),
        in_specs=in_specs,
        out_specs=out_specs,
        out_shape=out_shape,
        scratch_shapes=[pltpu.VMEM((tm, d), BF16)],
        compiler_params=_cparams(("arbitrary", "arbitrary", "arbitrary")),
        name="in_projection",
    )(*args)
    return (outs[0], outs[1]) if has_vres else (outs[0], None)


def _rglru_kernel(xa_ref, ga_ref, cw_ref, vec_ref, wa_ref, wx_ref, y_ref, xbuf, hc):
    t = pl.program_id(2)
    tt, cwid = xa_ref.shape

    @pl.when(t == 0)
    def _():
        xbuf[0:SUBLANES, :] = jnp.zeros((SUBLANES, cwid), F32)
        hc[...] = jnp.zeros_like(hc)

    xa = xa_ref[...]
    xbuf[SUBLANES:SUBLANES + tt, :] = xa
    cw = cw_ref[...]
    vec = vec_ref[...]
    xc = vec[0:1] + cw[0:1] * xa
    for j in range(1, CONV_A):
        xc = xc + cw[j:j + 1] * xbuf[SUBLANES - j:SUBLANES - j + tt, :]
    xbuf[0:SUBLANES, :] = xbuf[tt:tt + SUBLANES, :]

    xb = xc.astype(BF16)
    r = jax.nn.sigmoid(jnp.dot(xb, wa_ref[...], preferred_element_type=F32) + vec[1:2])
    i = jax.nn.sigmoid(jnp.dot(xb, wx_ref[...], preferred_element_type=F32) + vec[2:3])
    log_a = (-LRU_C) * r * _softplus(-vec[3:4])
    a = jnp.exp(log_a)
    b = jnp.sqrt(-jnp.tanh(log_a) * (a * a + 1.0)) * (i * xc)

    rows = lax.broadcasted_iota(jnp.int32, (tt, cwid), 0)
    step = 1
    while step < tt:
        a_sh = pltpu.roll(a, step, 0)
        b_sh = pltpu.roll(b, step, 0)
        m = rows >= step
        b = jnp.where(m, a * b_sh + b, b)
        a = jnp.where(m, a * a_sh, a)
        step *= 2
    h = a * hc[0:1, :] + b
    hc[0:1, :] = h[tt - 1:tt, :]

    g = ga_ref[...]
    gelu = 0.5 * g * (1.0 + jnp.tanh(0.7978845608028654 * (g + 0.044715 * (g * g * g))))
    y_ref[...] = (h * gelu).astype(BF16)


def _rglru(z, conv_w, vec, wa_bd, wx_bd):
    bsz, s, _ = z.shape
    tt, cwid = 256, 256
    nc = D_RNN // cwid
    return pl.pallas_call(
        _rglru_kernel,
        grid=(bsz, nc, s // tt),
        in_specs=[pl.BlockSpec((None, tt, cwid), lambda b, c, t: (b, t, OFF_XA // cwid + c)),
                  pl.BlockSpec((None, tt, cwid), lambda b, c, t: (b, t, OFF_GA // cwid + c)),
                  pl.BlockSpec((CONV_A, cwid), lambda b, c, t: (0, c)),
                  pl.BlockSpec((SUBLANES, cwid), lambda b, c, t: (0, c)),
                  pl.BlockSpec((None, cwid, cwid), lambda b, c, t: (c, 0, 0)),
                  pl.BlockSpec((None, cwid, cwid), lambda b, c, t: (c, 0, 0))],
        out_specs=pl.BlockSpec((None, tt, cwid), lambda b, c, t: (b, t, c)),
        out_shape=jax.ShapeDtypeStruct((bsz, s, D_RNN), BF16),
        scratch_shapes=[pltpu.VMEM((tt + SUBLANES, cwid), F32), pltpu.VMEM((SUBLANES, cwid), F32)],
        compiler_params=_cparams(("arbitrary", "arbitrary", "arbitrary")),
        name="rglru",
    )(z, z, conv_w, vec, wa_bd, wx_bd)


def _rope_kernel(pos_ref, freq_ref, z_ref, o_ref, cos_s, sin_a, sin_b):
    j = pl.program_id(2)
    tt = z_ref.shape[0]
    n_rot = 2 * (D_ATTN // (2 * LANES))

    @pl.when(j == 0)
    def _():
        ang = pos_ref[...].astype(F32) * freq_ref[...]
        lane = lax.broadcasted_iota(jnp.int32, (tt, LANES), 1) % HEAD_DIM
        cs = jnp.cos(ang)
        sn = jnp.sin(ang)
        half = ROT_DIM // 2
        cos_s[...] = cs
        sin_a[...] = jnp.where((lane >= half) & (lane < ROT_DIM), sn, 0.0)
        sin_b[...] = jnp.where(lane < half, -sn, 0.0)

    @pl.when(j < n_rot)
    def _():
        scale = jnp.where(j < n_rot // 2, HEAD_DIM ** -0.5, 1.0)
        half = ROT_DIM // 2
        for c in range(z_ref.shape[1] // LANES):
            sl = slice(c * LANES, (c + 1) * LANES)
            t = z_ref[:, sl]
            rot = (t * cos_s[...] + pltpu.roll(t, half, 1) * sin_a[...]
                   + pltpu.roll(t, LANES - half, 1) * sin_b[...])
            o_ref[:, sl] = (rot * scale).astype(BF16)

    @pl.when(j >= n_rot)
    def _():
        o_ref[...] = z_ref[...].astype(BF16)


def _rope(z, pos3, freq_lane):
    bsz, s, _ = z.shape
    tt, cwid = 512, 256
    nblk = 3 * D_ATTN // cwid
    return pl.pallas_call(
        _rope_kernel,
        grid=(bsz, s // tt, nblk),
        in_specs=[pl.BlockSpec((None, tt, 1), lambda b, t, j: (b, t, 0)),
                  pl.BlockSpec((1, LANES), lambda b, t, j: (0, 0)),
                  pl.BlockSpec((None, tt, cwid), lambda b, t, j: (b, t, OFF_QKV // cwid + j))],
        out_specs=pl.BlockSpec((None, tt, cwid), lambda b, t, j: (b, t, j)),
        out_shape=jax.ShapeDtypeStruct((bsz, s, 3 * D_ATTN), BF16),
        scratch_shapes=[pltpu.VMEM((tt, LANES), F32)] * 3,
        compiler_params=_cparams(("arbitrary", "arbitrary", "arbitrary")),
        name="rope",
    )(pos3, freq_lane, z)


def _attn_kernel(q_ref, kp_ref, kc_ref, vp_ref, vc_ref, o_ref, lse_ref):
    mb = pl.program_id(2)
    nq, width = q_ref.shape
    q = q_ref[...]
    k = jnp.concatenate([kp_ref[...], kc_ref[...]], axis=0)
    v = jnp.concatenate([vp_ref[...], vc_ref[...]], axis=0)
    nk = 2 * nq
    qi = lax.broadcasted_iota(jnp.int32, (nq, nk), 0)
    kj = lax.broadcasted_iota(jnp.int32, (nq, nk), 1)
    valid = (kj >= qi) & (kj <= qi + nq) & ((mb > 0) | (kj >= nq))
    head = lax.broadcasted_iota(jnp.int32, (nq, width), 1) // HEAD_DIM
    o = jnp.zeros((nq, width), F32)
    lse = jnp.zeros((nq, width), F32)
    for h in range(HEADS_PER_GROUP):
        hm = head == h
        qh = jnp.where(hm, q, jnp.zeros_like(q))
        s = lax.dot_general(qh, k, (((1,), (1,)), ((), ())), preferred_element_type=F32)
        s = jnp.where(valid, s, -1e30)
        m = jnp.max(s, axis=1, keepdims=True)
        p = jnp.exp(s - m)
        l = jnp.sum(p, axis=1, keepdims=True)
        pv = jnp.dot(p.astype(BF16), v, preferred_element_type=F32)
        o = jnp.where(hm, pv / l, o)
        lse = jnp.where(hm, m + jnp.log(l), lse)
    o_ref[...] = o
    lse_ref[...] = lse


def _attention_group(qkv, g, dil):
    bsz, s, ncol = qkv.shape
    sub = s // dil
    width = HEADS_PER_GROUP * HEAD_DIM
    per_row = ncol // width
    kq = D_ATTN // width
    view = qkv.reshape(bsz, sub, dil * ncol)
    nb = ATTN_BLOCK

    def qmap(off):
        return lambda b, r, m: (b, m, r * per_row + off * kq + g)

    def pmap(off):
        return lambda b, r, m: (b, jnp.maximum(m - 1, 0), r * per_row + off * kq + g)

    blk = (None, nb, width)
    o, lse = pl.pallas_call(
        _attn_kernel,
        grid=(bsz, dil, sub // nb),
        in_specs=[pl.BlockSpec(blk, qmap(0)),
                  pl.BlockSpec(blk, pmap(1)), pl.BlockSpec(blk, qmap(1)),
                  pl.BlockSpec(blk, pmap(2)), pl.BlockSpec(blk, qmap(2))],
        out_specs=[pl.BlockSpec(blk, lambda b, r, m: (b, m, r)),
                   pl.BlockSpec(blk, lambda b, r, m: (b, m, r))],
        out_shape=[jax.ShapeDtypeStruct((bsz, sub, dil * width), F32)] * 2,
        compiler_params=_cparams(("arbitrary", "arbitrary", "arbitrary")),
        name=f"dilated_attention_g{g}",
    )(view, view, view, view, view)
    return o.reshape(bsz, s, width), lse.reshape(bsz, s, width)


def _token_shift(z, halo, first, mu):
    prev = jnp.where(first, 0.0, halo[SUBLANES - 1:SUBLANES, :])
    zp = pltpu.roll(z, 1, 0)
    rows = lax.broadcasted_iota(jnp.int32, z.shape, 0)
    zp = jnp.where(rows == 0, prev, zp)
    return z + (zp - z) * mu


def _rwkv_prep_kernel(*refs, has_vres):
    if has_vres:
        (zr_ref, zk_ref, zv_ref, zl_ref, hr_ref, hk_ref, hv_ref, hl_ref, vec_ref, mul_ref,
         w2_ref, a2_ref, g2_ref, zv1_ref, hv1_ref, vf_ref, muv_ref, v2_ref,
         r_o, k_o, v_o, a_o, b_o, lw_o, g_o) = refs
    else:
        (zr_ref, zk_ref, zv_ref, zl_ref, hr_ref, hk_ref, hv_ref, hl_ref, vec_ref, mul_ref,
         w2_ref, a2_ref, g2_ref,
         r_o, k_o, v_o, a_o, b_o, lw_o, g_o) = refs
    first = pl.program_id(1) == 0
    vec = vec_ref[...]
    r = _token_shift(zr_ref[...], hr_ref[...], first, vec[0:1])
    k = _token_shift(zk_ref[...], hk_ref[...], first, vec[1:2])
    v = _token_shift(zv_ref[...], hv_ref[...], first, vec[2:3])
    lora = _token_shift(zl_ref[...], hl_ref[...], first, mul_ref[...])
    wa_in = lora[:, 0:LANES]
    w_pre = vec[3:4] + _dot(jnp.tanh(wa_in), w2_ref[...])
    w_log = -_softplus(-w_pre) - 0.5
    lw_o[...] = -jnp.exp(w_log)
    alpha = jax.nn.sigmoid(vec[4:5] + _dot(wa_in, a2_ref[...]))
    g_o[...] = _dot(jax.nn.sigmoid(lora[:, LANES:2 * LANES]), g2_ref[...])
    if has_vres:
        zv1 = _token_shift(zv1_ref[...], hv1_ref[...], first, muv_ref[...])
        mix = jax.nn.sigmoid(vec[7:8] + _dot(zv1, v2_ref[...]))
        v = v + (vf_ref[...] - v) * mix
    ones = _head_ones()
    kk = k * vec[5:6]
    kk = kk * lax.rsqrt(_head_sum(kk * kk, ones) + 1e-12)
    r_o[...] = r
    k_o[...] = k * (1.0 + (alpha - 1.0) * vec[6:7])
    v_o[...] = v
    a_o[...] = -kk
    b_o[...] = kk * alpha


def _rwkv_prep(z, zv1, v_first, vec, mu_l, w2p, a2p, g2, mu_v1, v2p):
    bsz, s, _ = z.shape
    tt = 256
    d = D_RWKV
    has_vres = zv1 is not None
    hstep = tt // SUBLANES

    def col(off, w):
        return lambda b, t: (b, t, off // w)

    def hcol(off, w):
        return lambda b, t: (b, jnp.maximum(t * hstep - 1, 0), off // w)

    lw = 2 * LANES
    in_specs = [pl.BlockSpec((None, tt, d), col(OFF_R, d)),
                pl.BlockSpec((None, tt, d), col(OFF_K, d)),
                pl.BlockSpec((None, tt, d), col(OFF_V, d)),
                pl.BlockSpec((None, tt, lw), col(OFF_LORA, lw)),
                pl.BlockSpec((None, SUBLANES, d), hcol(OFF_R, d)),
                pl.BlockSpec((None, SUBLANES, d), hcol(OFF_K, d)),
                pl.BlockSpec((None, SUBLANES, d), hcol(OFF_V, d)),
                pl.BlockSpec((None, SUBLANES, lw), hcol(OFF_LORA, lw)),
                pl.BlockSpec((SUBLANES, d), lambda b, t: (0, 0)),
                pl.BlockSpec((1, lw), lambda b, t: (0, 0)),
                pl.BlockSpec((LANES, d), lambda b, t: (0, 0)),
                pl.BlockSpec((LANES, d), lambda b, t: (0, 0)),
                pl.BlockSpec((LANES, d), lambda b, t: (0, 0))]
    args = [z, z, z, z, z, z, z, z, vec, mu_l, w2p, a2p, g2]
    if has_vres:
        in_specs += [pl.BlockSpec((None, tt, LANES), lambda b, t: (b, t, 0)),
                     pl.BlockSpec((None, SUBLANES, LANES),
                                  lambda b, t: (b, jnp.maximum(t * hstep - 1, 0), 0)),
                     pl.BlockSpec((None, tt, d), lambda b, t: (b, t, 0)),
                     pl.BlockSpec((1, LANES), lambda b, t: (0, 0)),
                     pl.BlockSpec((LANES, d), lambda b, t: (0, 0))]
        args += [zv1, zv1, v_first, mu_v1, v2p]
    out_spec = pl.BlockSpec((None, tt, d), lambda b, t: (b, t, 0))
    return pl.pallas_call(
        functools.partial(_rwkv_prep_kernel, has_vres=has_vres),
        grid=(bsz, s // tt),
        in_specs=in_specs,
        out_specs=[out_spec] * 7,
        out_shape=[jax.ShapeDtypeStruct((bsz, s, d), F32)] * 7,
        compiler_params=_cparams(("arbitrary", "arbitrary")),
        name="rwkv_prep",
    )(*args)


def _wkv_kernel(r_ref, k_ref, v_ref, a_ref, b_ref, lw_ref, g_ref, vec_ref, y_ref, s_scr):
    L = r_ref.shape[0]
    n_pairs = r_ref.shape[1] // LANES

    @pl.when(pl.program_id(1) == 0)
    def _():
        s_scr[...] = jnp.zeros_like(s_scr)

    lw = lw_ref[...]
    ti = lax.broadcasted_iota(jnp.int32, (L, L), 0)
    tj = lax.broadcasted_iota(jnp.int32, (L, L), 1)
    tri = jnp.where(ti >= tj, 1.0, 0.0).astype(BF16)
    l0, l1, l2 = _split3(lw)
    lc = (jnp.dot(tri, l0, preferred_element_type=F32) + jnp.dot(tri, l1, preferred_element_type=F32)
          + jnp.dot(tri, l2, preferred_element_type=F32))
    lc_last = lc[L - 1:L, :]
    r = r_ref[...]
    k = k_ref[...]
    v = v_ref[...]
    a = a_ref[...]
    b = b_ref[...]
    e_inv = jnp.exp(-lc)
    e_tail = jnp.exp(lc_last - lc)
    a_hat = a * jnp.exp(lc - lw)
    r_hat = r * jnp.exp(lc)
    b_hat = b * e_inv
    k_hat = k * e_inv
    b_til = b * e_tail
    k_til = k * e_tail
    d_last = jnp.exp(lc_last)

    lane = lax.broadcasted_iota(jnp.int32, (L, LANES), 1)
    even = lane < HEAD_DIM

    def cat(x):
        return jnp.concatenate([jnp.where(even, x, 0.0), jnp.where(even, 0.0, x)], axis=0)

    n2 = 2 * L
    pi = lax.broadcasted_iota(jnp.int32, (n2, n2), 0)
    qi = lax.broadcasted_iota(jnp.int32, (n2, n2), 1)
    same = (pi // L) == (qi // L)
    strict = same & ((pi % L) > (qi % L))
    incl = same & ((pi % L) >= (qi % L))
    eye = jnp.where(pi == qi, 1.0, 0.0)

    ys = []
    for p in range(n_pairs):
        sl = slice(p * LANES, (p + 1) * LANES)
        ah_c = cat(a_hat[:, sl])
        rh_c = cat(r_hat[:, sl])
        v_c = cat(v[:, sl])
        lhs = jnp.concatenate([ah_c, rh_c], axis=0)
        rhs = jnp.concatenate([cat(b_hat[:, sl]), cat(k_hat[:, sl])], axis=0)
        gram = _dot_nt(lhs, rhs)
        nmat = jnp.where(strict, gram[:n2, :n2], 0.0)
        a_ak = jnp.where(strict, gram[:n2, n2:], 0.0)
        a_rb = jnp.where(incl, gram[n2:, :n2], 0.0)
        a_rk = jnp.where(incl, gram[n2:, n2:], 0.0)
        tinv = eye + nmat
        npow = nmat
        span = 2
        while span < L:
            npow = _dot(npow, npow)
            tinv = tinv + _dot(tinv, npow)
            span *= 2
        x1 = _dot(a_ak, v_c)
        wu = _dot(tinv, jnp.concatenate([ah_c, x1], axis=1))
        qy = _dot(a_rb, wu)
        q_c = rh_c + qy[:, :LANES]
        y0 = qy[:, LANES:] + _dot(a_rk, v_c)
        mc = _dot_tn(cat(b_til[:, sl]), wu)
        s_prev = s_scr[p]
        y_c = _dot(q_c, s_prev) + y0
        ys.append(y_c[:L] + y_c[L:])
        dl = jnp.broadcast_to(d_last[:, sl], (n2, n2))
        m_mat = mc[:, :LANES] + jnp.where(pi == qi, dl, 0.0)
        c_mat = mc[:, LANES:] + _dot_tn(cat(k_til[:, sl]), v_c)
        s_scr[p] = _dot(m_mat, s_prev) + c_mat

    y = jnp.concatenate(ys, axis=1)
    vec = vec_ref[...]
    ones = _head_ones()
    inv_n = 1.0 / HEAD_DIM
    mu = _head_sum(y, ones) * inv_n
    dy = y - mu
    var = _head_sum(dy * dy, ones) * inv_n
    gn = dy * lax.rsqrt(var + RWKV_GN_EPS) * vec[1:2] + vec[2:3]
    bonus = _head_sum(r * k * vec[0:1], ones) * v
    y_ref[...] = ((gn + bonus) * g_ref[...]).astype(BF16)


def _wkv(r, k, v, a, b, lw, g, vec):
    bsz, s, d = r.shape
    L = WKV_CHUNK
    spec = pl.BlockSpec((None, L, d), lambda bb, c: (bb, c, 0))
    return pl.pallas_call(
        _wkv_kernel,
        grid=(bsz, s // L),
        in_specs=[spec] * 7 + [pl.BlockSpec((SUBLANES, d), lambda bb, c: (0, 0))],
        out_specs=spec,
        out_shape=jax.ShapeDtypeStruct((bsz, s, d), BF16),
        scratch_shapes=[pltpu.VMEM((d // LANES, LANES, LANES), F32)],
        compiler_params=_cparams(("arbitrary", "arbitrary")),
        name="wkv7",
    )(r, k, v, a, b, lw, g, vec)


def _merge_kernel(x_ref, ya_ref, yc_ref, o0_ref, o1_ref, o2_ref, l0_ref, l1_ref, l2_ref,
                  ga_ref, gb_ref, gc_ref, gt_ref, pa_ref, pb_ref, pc_ref, wo_ref, ln_ref, out_ref):
    l0 = l0_ref[...]
    l1 = l1_ref[...]
    l2 = l2_ref[...]
    lm = jnp.maximum(jnp.maximum(l0, l1), l2)
    e0 = jnp.exp(l0 - lm)
    e1 = jnp.exp(l1 - lm)
    e2 = jnp.exp(l2 - lm)
    yb = (e0 * o0_ref[...] + e1 * o1_ref[...] + e2 * o2_ref[...]) / (e0 + e1 + e2)
    merged = (jax.nn.sigmoid(ga_ref[...]) * jnp.dot(ya_ref[...], pa_ref[...], preferred_element_type=F32)
              + jax.nn.sigmoid(gb_ref[...]) * _dot(yb, pb_ref[...])
              + jax.nn.sigmoid(gc_ref[...]) * jnp.dot(yc_ref[...], pc_ref[...], preferred_element_type=F32))
    y = _dot(merged, wo_ref[...])
    ln = ln_ref[...]
    out_ref[...] = _layer_norm(ALPHA * x_ref[...] + (1.0 + gt_ref[...]) * y, ln[0:1], ln[1:2])


def _merge(x, z, ya, yc, att, gt, pa, pb, pc, wo, ln):
    bsz, s, d = x.shape
    tm = 256
    wa = D_ATTN_OUT
    row = lambda w, off=0: pl.BlockSpec((None, tm, w), lambda b, i: (b, i, off // w))
    const = lambda shp: pl.BlockSpec(shp, lambda b, i: (0,) * len(shp))
    (o0, l0), (o1, l1), (o2, l2) = att
    return pl.pallas_call(
        _merge_kernel,
        grid=(bsz, s // tm),
        in_specs=[row(d), row(d), row(d)] + [row(wa)] * 6
                 + [row(d, OFF_GATE), row(d, OFF_GATE + d), row(d, OFF_GATE + 2 * d),
                    pl.BlockSpec((None, 1, d), lambda b, i: (b, 0, 0)),
                    const((d, d)), const((wa, d)), const((d, d)), const((d, d)),
                    const((SUBLANES, d))],
        out_specs=row(d),
        out_shape=jax.ShapeDtypeStruct((bsz, s, d), F32),
        compiler_params=_cparams(("arbitrary", "arbitrary")),
        name="merge_proj_ln",
    )(x, ya, yc, o0, o1, o2, l0, l1, l2, z, z, z, gt, pa, pb, pc, wo, ln)


def _ffn_kernel(x_ref, halo_ref, sc_ref, sh_ref, gt_ref, wg_ref, wv_ref, cwg_ref, cwv_ref,
                wd_ref, ln_ref, out_ref, h_scr, ug_scr, uv_scr, acc):
    i = pl.program_id(1)
    f = pl.program_id(2)
    tm = x_ref.shape[0]
    hr = halo_ref.shape[0]

    @pl.when(f == 0)
    def _():
        sc = 1.0 + sc_ref[...]
        sh = sh_ref[...]
        hh = jnp.where(i == 0, 0.0, halo_ref[...] * sc + sh)
        h_scr[0:hr, :] = hh.astype(BF16)
        h_scr[hr:hr + tm, :] = (x_ref[...] * sc + sh).astype(BF16)

    h = h_scr[...]
    ug_scr[...] = jnp.dot(h, wg_ref[...], preferred_element_type=F32)
    uv_scr[...] = jnp.dot(h, wv_ref[...], preferred_element_type=F32)

    def conv(u_scr, cw):
        out = cw[CONV_F:CONV_F + 1]
        for j in range(CONV_F):
            out = out + cw[j:j + 1] * u_scr[hr - j:hr - j + tm, :]
        return out

    cg = conv(ug_scr, cwg_ref[...])
    cv = conv(uv_scr, cwv_ref[...])
    act = cg * jax.nn.sigmoid(cg) * cv
    part = _dot(act, wd_ref[...])

    @pl.when(f == 0)
    def _():
        acc[...] = part

    @pl.when(f > 0)
    def _():
        acc[...] += part

    @pl.when(f == pl.num_programs(2) - 1)
    def _():
        ln = ln_ref[...]
        out_ref[...] = _layer_norm(ALPHA * x_ref[...] + (1.0 + gt_ref[...]) * acc[...],
                                   ln[0:1], ln[1:2])


def _conv_ffn(x, sc, sh, gt, w_up, conv_wb, w_down, ln):
    bsz, s, d = x.shape
    tm, fc = 512, 1408
    nf = D_FF // fc
    hr = BF16_ROWS
    mod = pl.BlockSpec((None, 1, d), lambda b, i, f: (b, 0, 0))
    return pl.pallas_call(
        _ffn_kernel,
        grid=(bsz, s // tm, nf),
        in_specs=[pl.BlockSpec((None, tm, d), lambda b, i, f: (b, i, 0)),
                  pl.BlockSpec((None, hr, d), lambda b, i, f: (b, jnp.maximum(i * (tm // hr) - 1, 0), 0)),
                  mod, mod, mod,
                  pl.BlockSpec((d, fc), lambda b, i, f: (0, f)),
                  pl.BlockSpec((d, fc), lambda b, i, f: (0, nf + f)),
                  pl.BlockSpec((SUBLANES, fc), lambda b, i, f: (0, f)),
                  pl.BlockSpec((SUBLANES, fc), lambda b, i, f: (0, nf + f)),
                  pl.BlockSpec((fc, d), lambda b, i, f: (f, 0)),
                  pl.BlockSpec((SUBLANES, d), lambda b, i, f: (0, 0))],
        out_specs=pl.BlockSpec((None, tm, d), lambda b, i, f: (b, i, 0)),
        out_shape=jax.ShapeDtypeStruct((bsz, s, d), F32),
        scratch_shapes=[pltpu.VMEM((tm + hr, d), BF16),
                        pltpu.VMEM((tm + hr, fc), F32),
                        pltpu.VMEM((tm + hr, fc), F32),
                        pltpu.VMEM((tm, d), F32)],
        compiler_params=_cparams(("arbitrary", "arbitrary", "arbitrary")),
        name="conv_ffn_ln",
    )(x, x, sc, sh, gt, w_up, w_up, conv_wb, conv_wb, w_down, ln)


def _rows(vectors, n_rows=SUBLANES):
    m = jnp.stack([v.astype(F32) for v in vectors], axis=0)
    return jnp.pad(m, ((0, n_rows - m.shape[0]), (0, 0)))


def _block_diag(w, per):
    g, n, _ = w.shape
    w = w.reshape(g // per, per, n, n)
    eye = jnp.eye(per, dtype=w.dtype)
    out = jnp.einsum('cpij,pq->cpiqj', w, eye)
    return out.reshape(g // per, per * n, per * n)


def kernel(x, c, positions, mod_w, mod_b, w_in, w_in_vres, conv_a_w, conv_a_b, lru_wa, lru_ba, lru_wx, lru_bx, lru_lambda, rwkv_mu, mu_vres, w0, w2, a0, a2, g2, v0, v2, k_k, k_a, r_k, ln_x_w, ln_x_b, proj_a, proj_b, proj_c, w_o, ln1_w, ln1_b, ffn_up, ffn_conv_w, ffn_conv_b, ffn_down, ln2_w, ln2_b):
    bsz, s, d = x.shape
    mod = _modulation(c, mod_w, mod_b)
    half = ROT_DIM // 2
    inv_freq = ROPE_THETA ** (-jnp.arange(half, dtype=F32) / half)
    lane = jnp.arange(LANES)
    freq_lane = jnp.where((lane % HEAD_DIM) < ROT_DIM, inv_freq[lane % half], 0.0).reshape(1, LANES)
    pos3 = positions.reshape(bsz, s, 1)
    zero_d = jnp.zeros((d,), F32)

    v_first = None
    for l in range(DEPTH):
        m6 = mod[l].reshape(bsz, 6, 1, d)
        sh1, sc1, gt1, sh2, sc2, gt2 = (m6[:, i] for i in range(6))

        wl = w_in[l]
        w_perm = jnp.concatenate([wl[:, 0:2048], wl[:, 4352:7424], wl[:, 7680:10752],
                                  wl[:, 2048:4352], wl[:, 7424:7680]], axis=1).astype(BF16)
        if l == 0:
            w_vres = None
        else:
            w_vres = jnp.pad(w_in_vres[l - 1], ((0, 0), (0, LANES - MV_LORA))).astype(BF16)
        z, zv1 = _in_projection(x, sc1, sh1, w_perm, w_vres)

        per = 256 // RNN_BLOCK
        y_a = _rglru(z, conv_a_w[l],
                     _rows([conv_a_b[l], lru_ba[l], lru_bx[l], lru_lambda[l]]),
                     _block_diag(lru_wa[l], per).astype(BF16),
                     _block_diag(lru_wx[l], per).astype(BF16))

        qkv = _rope(z, pos3, freq_lane)
        att = [_attention_group(qkv, g, dil) for g, (_, dil) in enumerate(ATTN_GROUPS)]

        mu = rwkv_mu[l]
        vec = _rows([mu[0:1024], mu[1024:2048], mu[2048:3072], w0[l], a0[l], k_k[l], k_a[l],
                     v0[l - 1] if l > 0 else zero_d])
        mu_l = mu[3072:3328].reshape(1, 2 * LANES)
        w2p = jnp.pad(w2[l], ((0, LANES - LORA_W), (0, 0))).astype(BF16)
        a2p = jnp.pad(a2[l], ((LORA_W, 0), (0, 0))).astype(BF16)
        if l == 0:
            prep = _rwkv_prep(z, None, None, vec, mu_l, w2p, a2p, g2[l].astype(BF16), None, None)
        else:
            mu_v1 = jnp.pad(mu_vres[l - 1], (0, LANES - MV_LORA)).reshape(1, LANES)
            v2p = jnp.pad(v2[l - 1], ((0, LANES - MV_LORA), (0, 0))).astype(BF16)
            prep = _rwkv_prep(z, zv1, v_first, vec, mu_l, w2p, a2p, g2[l].astype(BF16), mu_v1, v2p)
        r_, k_, v_, a_, b_, lw_, g_ = prep
        if l == 0:
            v_first = v_
        y_c = _wkv(r_, k_, v_, a_, b_, lw_, g_,
                   _rows([r_k[l].reshape(-1), ln_x_w[l], ln_x_b[l]]))

        x = _merge(x, z, y_a, y_c, att, gt1, proj_a[l].astype(BF16), proj_b[l].astype(BF16),
                   proj_c[l].astype(BF16), w_o[l].astype(BF16), _rows([ln1_w[l], ln1_b[l]]))

        conv_wb = jnp.concatenate([ffn_conv_w[l], ffn_conv_b[l][None, :]], axis=0)
        conv_wb = jnp.pad(conv_wb, ((0, SUBLANES - conv_wb.shape[0]), (0, 0)))
        x = _conv_ffn(x, sc2, sh2, gt2, ffn_up[l].astype(BF16), conv_wb,
                      ffn_down[l].astype(BF16), _rows([ln2_w[l], ln2_b[l]]))
    return x
```

```python
import functools

import jax
import jax.numpy as jnp
from jax import lax
from jax.experimental import pallas as pl
from jax.experimental.pallas import tpu as pltpu

F32 = jnp.float32
BF16 = jnp.bfloat16

D_MODEL = 1024
DEPTH = 2
D_RNN = 1024
RNN_BLOCK = 64
CONV_A = 4
LRU_C = 8.0
HEAD_DIM = 64
ATTN_GROUPS = ((128, 1), (512, 4), (2048, 16))
HEADS_PER_GROUP = 4
N_GROUPS = 3
D_ATTN = 768
D_ATTN_OUT = 256
ROT_DIM = 16
ROPE_THETA = 500000.0
D_RWKV = 1024
LORA_W = 64
LORA_A = 64
LORA_G = 128
MV_LORA = 32
RWKV_GN_EPS = 64e-5
D_FF = 2816
CONV_F = 3
ALPHA = (2 * DEPTH) ** 0.25
LN_EPS = 1e-5

LANES = 128
SUBLANES = 8
BF16_ROWS = 16

OFF_XA = 0
OFF_GA = 1024
OFF_R = 2048
OFF_K = 3072
OFF_V = 4096
OFF_GATE = 5120
OFF_QKV = 8192
OFF_LORA = 10496
N_Z = 10752

WKV_CHUNK = 64
ATTN_BLOCK = 128
VMEM_LIMIT = 52 * 1024 * 1024


def _cparams(sem):
    return pltpu.CompilerParams(dimension_semantics=sem, vmem_limit_bytes=VMEM_LIMIT)


def _dot(a, b):
    return jnp.dot(a.astype(BF16), b.astype(BF16), preferred_element_type=F32)


def _dot_nt(a, b):
    return lax.dot_general(a.astype(BF16), b.astype(BF16), (((1,), (1,)), ((), ())),
                           preferred_element_type=F32)


def _dot_tn(a, b):
    return lax.dot_general(a.astype(BF16), b.astype(BF16), (((0,), (0,)), ((), ())),
                           preferred_element_type=F32)


def _split3(x):
    p0 = x.astype(BF16)
    r1 = x - p0.astype(F32)
    p1 = r1.astype(BF16)
    p2 = (r1 - p1.astype(F32)).astype(BF16)
    return p0, p1, p2


def _head_ones():
    i = lax.broadcasted_iota(jnp.int32, (LANES, LANES), 0) // HEAD_DIM
    j = lax.broadcasted_iota(jnp.int32, (LANES, LANES), 1) // HEAD_DIM
    return jnp.where(i == j, 1.0, 0.0).astype(BF16)


def _head_sum(x, ones):
    p0, p1, p2 = _split3(x)
    outs = []
    for c in range(x.shape[1] // LANES):
        sl = slice(c * LANES, (c + 1) * LANES)
        outs.append(jnp.dot(p0[:, sl], ones, preferred_element_type=F32)
                    + jnp.dot(p1[:, sl], ones, preferred_element_type=F32)
                    + jnp.dot(p2[:, sl], ones, preferred_element_type=F32))
    return outs[0] if len(outs) == 1 else jnp.concatenate(outs, axis=1)


def _softplus(x):
    return jnp.maximum(x, 0.0) + jnp.log1p(jnp.exp(-jnp.abs(x)))


def _layer_norm(x, w, b):
    mu = jnp.mean(x, axis=-1, keepdims=True)
    d = x - mu
    var = jnp.mean(d * d, axis=-1, keepdims=True)
    return d * lax.rsqrt(var + LN_EPS) * w + b


def _mod_kernel(c_ref, w_ref, b_ref, o_ref):
    c = c_ref[...]
    s = c * jax.nn.sigmoid(c)
    s0, s1, s2 = _split3(s)
    w0, w1, w2 = _split3(w_ref[...])
    acc = jnp.dot(s0, w0, preferred_element_type=F32)
    acc += jnp.dot(s0, w1, preferred_element_type=F32) + jnp.dot(s1, w0, preferred_element_type=F32)
    acc += (jnp.dot(s1, w1, preferred_element_type=F32) + jnp.dot(s0, w2, preferred_element_type=F32)
            + jnp.dot(s2, w0, preferred_element_type=F32))
    o_ref[...] = acc + b_ref[...]


def _modulation(c, mod_w, mod_b):
    depth, d, n = mod_w.shape
    bsz = c.shape[0]
    rows = BF16_ROWS
    tn = 1536
    c_pad = jnp.pad(c, ((0, rows - bsz), (0, 0)))
    out = pl.pallas_call(
        _mod_kernel,
        grid=(depth, n // tn),
        in_specs=[pl.BlockSpec((rows, d), lambda l, j: (0, 0)),
                  pl.BlockSpec((None, d, tn), lambda l, j: (l, 0, j)),
                  pl.BlockSpec((None, 1, tn), lambda l, j: (l, 0, j))],
        out_specs=pl.BlockSpec((None, rows, tn), lambda l, j: (l, 0, j)),
        out_shape=jax.ShapeDtypeStruct((depth, rows, n), F32),
        compiler_params=_cparams(("arbitrary", "arbitrary")),
        name="adaln_mod",
    )(c_pad, mod_w, mod_b.reshape(depth, 1, n))
    return out[:, :bsz]


def _inproj_kernel(x_ref, sc_ref, sh_ref, w_ref, *rest, has_vres):
    if has_vres:
        wv_ref, z_ref, zv_ref, h_scr = rest
    else:
        z_ref, h_scr = rest

    @pl.when(pl.program_id(2) == 0)
    def _():
        h_scr[...] = (x_ref[...] * (1.0 + sc_ref[...]) + sh_ref[...]).astype(BF16)
        if has_vres:
            zv_ref[...] = jnp.dot(h_scr[...], wv_ref[...], preferred_element_type=F32)

    z_ref[...] = jnp.dot(h_scr[...], w_ref[...], preferred_element_type=F32)


def _in_projection(x, sc, sh, w, w_vres):
    bsz, s, d = x.shape
    n = w.shape[1]
    tm, tn = 1024, 1536
    has_vres = w_vres is not None
    in_specs = [pl.BlockSpec((None, tm, d), lambda b, i, j: (b, i, 0)),
                pl.BlockSpec((None, 1, d), lambda b, i, j: (b, 0, 0)),
                pl.BlockSpec((None, 1, d), lambda b, i, j: (b, 0, 0)),
                pl.BlockSpec((d, tn), lambda b, i, j: (0, j))]
    out_specs = [pl.BlockSpec((None, tm, tn), lambda b, i, j: (b, i, j))]
    out_shape = [jax.ShapeDtypeStruct((bsz, s, n), F32)]
    args = [x, sc, sh, w]
    if has_vres:
        in_specs.append(pl.BlockSpec((d, LANES), lambda b, i, j: (0, 0)))
        out_specs.append(pl.BlockSpec((None, tm, LANES), lambda b, i, j: (b, i, 0)))
        out_shape.append(jax.ShapeDtypeStruct((bsz, s, LANES), F32))
        args.append(w_vres)
    outs = pl.pallas_call(
        functools.partial(_inproj_kernel, has_vres=has_vres),
        grid=(bsz, s // tm, n // tn),
        in_specs=in_specs,
        out_specs=out_specs,
        out_shape=out_shape,
        scratch_shapes=[pltpu.VMEM((tm, d), BF16)],
        compiler_params=_cparams(("arbitrary", "arbitrary", "arbitrary")),
        name="in_projection",
    )(*args)
    return (outs[0], outs[1]) if has_vres else (outs[0], None)


def _rglru_kernel(xa_ref, ga_ref, cw_ref, vec_ref, wa_ref, wx_ref, y_ref, xbuf, hc):
    t = pl.program_id(2)
    tt, cwid = xa_ref.shape

    @pl.when(t == 0)
    def _():
        xbuf[0:SUBLANES, :] = jnp.zeros((SUBLANES, cwid), F32)
        hc[...] = jnp.zeros_like(hc)

    xa = xa_ref[...]
    xbuf[SUBLANES:SUBLANES + tt, :] = xa
    cw = cw_ref[...]
    vec = vec_ref[...]
    xc = vec[0:1] + cw[0:1] * xa
    for j in range(1, CONV_A):
        xc = xc + cw[j:j + 1] * xbuf[SUBLANES - j:SUBLANES - j + tt, :]
    xbuf[0:SUBLANES, :] = xbuf[tt:tt + SUBLANES, :]

    xb = xc.astype(BF16)
    r = jax.nn.sigmoid(jnp.dot(xb, wa_ref[...], preferred_element_type=F32) + vec[1:2])
    i = jax.nn.sigmoid(jnp.dot(xb, wx_ref[...], preferred_element_type=F32) + vec[2:3])
    log_a = (-LRU_C) * r * _softplus(-vec[3:4])
    a = jnp.exp(log_a)
    b = jnp.sqrt(-jnp.tanh(log_a) * (a * a + 1.0)) * (i * xc)

    rows = lax.broadcasted_iota(jnp.int32, (tt, cwid), 0)
    step = 1
    while step < tt:
        a_sh = pltpu.roll(a, step, 0)
        b_sh = pltpu.roll(b, step, 0)
        m = rows >= step
        b = jnp.where(m, a * b_sh + b, b)
        a = jnp.where(m, a * a_sh, a)
        step *= 2
    h = a * hc[0:1, :] + b
    hc[0:1, :] = h[tt - 1:tt, :]

    g = ga_ref[...]
    gelu = 0.5 * g * (1.0 + jnp.tanh(0.7978845608028654 * (g + 0.044715 * (g * g * g))))
    y_ref[...] = (h * gelu).astype(BF16)


def _rglru(z, conv_w, vec, wa_bd, wx_bd):
    bsz, s, _ = z.shape
    tt, cwid = 256, 256
    nc = D_RNN // cwid
    return pl.pallas_call(
        _rglru_kernel,
        grid=(bsz, nc, s // tt),
        in_specs=[pl.BlockSpec((None, tt, cwid), lambda b, c, t: (b, t, OFF_XA // cwid + c)),
                  pl.BlockSpec((None, tt, cwid), lambda b, c, t: (b, t, OFF_GA // cwid + c)),
                  pl.BlockSpec((CONV_A, cwid), lambda b, c, t: (0, c)),
                  pl.BlockSpec((SUBLANES, cwid), lambda b, c, t: (0, c)),
                  pl.BlockSpec((None, cwid, cwid), lambda b, c, t: (c, 0, 0)),
                  pl.BlockSpec((None, cwid, cwid), lambda b, c, t: (c, 0, 0))],
        out_specs=pl.BlockSpec((None, tt, cwid), lambda b, c, t: (b, t, c)),
        out_shape=jax.ShapeDtypeStruct((bsz, s, D_RNN), BF16),
        scratch_shapes=[pltpu.VMEM((tt + SUBLANES, cwid), F32), pltpu.VMEM((SUBLANES, cwid), F32)],
        compiler_params=_cparams(("arbitrary", "arbitrary", "arbitrary")),
        name="rglru",
    )(z, z, conv_w, vec, wa_bd, wx_bd)


def _rope_kernel(pos_ref, freq_ref, z_ref, o_ref, cos_s, sin_a, sin_b):
    j = pl.program_id(2)
    tt = z_ref.shape[0]
    n_rot = 2 * (D_ATTN // (2 * LANES))

    @pl.when(j == 0)
    def _():
        ang = pos_ref[...].astype(F32) * freq_ref[...]
        lane = lax.broadcasted_iota(jnp.int32, (tt, LANES), 1) % HEAD_DIM
        cs = jnp.cos(ang)
        sn = jnp.sin(ang)
        half = ROT_DIM // 2
        cos_s[...] = cs
        sin_a[...] = jnp.where((lane >= half) & (lane < ROT_DIM), sn, 0.0)
        sin_b[...] = jnp.where(lane < half, -sn, 0.0)

    @pl.when(j < n_rot)
    def _():
        scale = jnp.where(j < n_rot // 2, HEAD_DIM ** -0.5, 1.0)
        half = ROT_DIM // 2
        for c in range(z_ref.shape[1] // LANES):
            sl = slice(c * LANES, (c + 1) * LANES)
            t = z_ref[:, sl]
            rot = (t * cos_s[...] + pltpu.roll(t, half, 1) * sin_a[...]
                   + pltpu.roll(t, LANES - half, 1) * sin_b[...])
            o_ref[:, sl] = (rot * scale).astype(BF16)

    @pl.when(j >= n_rot)
    def _():
        o_ref[...] = z_ref[...].astype(BF16)


def _rope(z, pos3, freq_lane):
    bsz, s, _ = z.shape
    tt, cwid = 512, 256
    nblk = 3 * D_ATTN // cwid
    return pl.pallas_call(
        _rope_kernel,
        grid=(bsz, s // tt, nblk),
        in_specs=[pl.BlockSpec((None, tt, 1), lambda b, t, j: (b, t, 0)),
                  pl.BlockSpec((1, LANES), lambda b, t, j: (0, 0)),
                  pl.BlockSpec((None, tt, cwid), lambda b, t, j: (b, t, OFF_QKV // cwid + j))],
        out_specs=pl.BlockSpec((None, tt, cwid), lambda b, t, j: (b, t, j)),
        out_shape=jax.ShapeDtypeStruct((bsz, s, 3 * D_ATTN), BF16),
        scratch_shapes=[pltpu.VMEM((tt, LANES), F32)] * 3,
        compiler_params=_cparams(("arbitrary", "arbitrary", "arbitrary")),
        name="rope",
    )(pos3, freq_lane, z)


def _attn_kernel(q_ref, kp_ref, kc_ref, vp_ref, vc_ref, o_ref, lse_ref):
    mb = pl.program_id(2)
    nq, width = q_ref.shape
    q = q_ref[...]
    k = jnp.concatenate([kp_ref[...], kc_ref[...]], axis=0)
    v = jnp.concatenate([vp_ref[...], vc_ref[...]], axis=0)
    nk = 2 * nq
    qi = lax.broadcasted_iota(jnp.int32, (nq, nk), 0)
    kj = lax.broadcasted_iota(jnp.int32, (nq, nk), 1)
    valid = (kj >= qi) & (kj <= qi + nq) & ((mb > 0) | (kj >= nq))
    head = lax.broadcasted_iota(jnp.int32, (nq, width), 1) // HEAD_DIM
    o = jnp.zeros((nq, width), F32)
    lse = jnp.zeros((nq, width), F32)
    for h in range(HEADS_PER_GROUP):
        hm = head == h
        qh = jnp.where(hm, q, jnp.zeros_like(q))
        s = lax.dot_general(qh, k, (((1,), (1,)), ((), ())), preferred_element_type=F32)
        s = jnp.where(valid, s, -1e30)
        m = jnp.max(s, axis=1, keepdims=True)
        p = jnp.exp(s - m)
        l = jnp.sum(p, axis=1, keepdims=True)
        pv = jnp.dot(p.astype(BF16), v, preferred_element_type=F32)
        o = jnp.where(hm, pv / l, o)
        lse = jnp.where(hm, m + jnp.log(l), lse)
    o_ref[...] = o
    lse_ref[...] = lse


def _attention_group(qkv, g, dil):
    bsz, s, ncol = qkv.shape
    sub = s // dil
    width = HEADS_PER_GROUP * HEAD_DIM
    per_row = ncol // width
    kq = D_ATTN // width
    view = qkv.reshape(bsz, sub, dil * ncol)
    nb = ATTN_BLOCK

    def qmap(off):
        return lambda b, r, m: (b, m, r * per_row + off * kq + g)

    def pmap(off):
        return lambda b, r, m: (b, jnp.maximum(m - 1, 0), r * per_row + off * kq + g)

    blk = (None, nb, width)
    o, lse = pl.pallas_call(
        _attn_kernel,
        grid=(bsz, dil, sub // nb),
        in_specs=[pl.BlockSpec(blk, qmap(0)),
                  pl.BlockSpec(blk, pmap(1)), pl.BlockSpec(blk, qmap(1)),
                  pl.BlockSpec(blk, pmap(2)), pl.BlockSpec(blk, qmap(2))],
        out_specs=[pl.BlockSpec(blk, lambda b, r, m: (b, m, r)),
                   pl.BlockSpec(blk, lambda b, r, m: (b, m, r))],
        out_shape=[jax.ShapeDtypeStruct((bsz, sub, dil * width), F32)] * 2,
        compiler_params=_cparams(("arbitrary", "arbitrary", "arbitrary")),
        name=f"dilated_attention_g{g}",
    )(view, view, view, view, view)
    return o.reshape(bsz, s, width), lse.reshape(bsz, s, width)


def _token_shift(z, halo, first, mu):
    prev = jnp.where(first, 0.0, halo[SUBLANES - 1:SUBLANES, :])
    zp = pltpu.roll(z, 1, 0)
    rows = lax.broadcasted_iota(jnp.int32, z.shape, 0)
    zp = jnp.where(rows == 0, prev, zp)
    return z + (zp - z) * mu


def _rwkv_prep_kernel(*refs, has_vres):
    if has_vres:
        (zr_ref, zk_ref, zv_ref, zl_ref, hr_ref, hk_ref, hv_ref, hl_ref, vec_ref, mul_ref,
         w2_ref, a2_ref, g2_ref, zv1_ref, hv1_ref, vf_ref, muv_ref, v2_ref,
         r_o, k_o, v_o, a_o, b_o, lw_o, g_o) = refs
    else:
        (zr_ref, zk_ref, zv_ref, zl_ref, hr_ref, hk_ref, hv_ref, hl_ref, vec_ref, mul_ref,
         w2_ref, a2_ref, g2_ref,
         r_o, k_o, v_o, a_o, b_o, lw_o, g_o) = refs
    first = pl.program_id(1) == 0
    vec = vec_ref[...]
    r = _token_shift(zr_ref[...], hr_ref[...], first, vec[0:1])
    k = _token_shift(zk_ref[...], hk_ref[...], first, vec[1:2])
    v = _token_shift(zv_ref[...], hv_ref[...], first, vec[2:3])
    lora = _token_shift(zl_ref[...], hl_ref[...], first, mul_ref[...])
    wa_in = lora[:, 0:LANES]
    w_pre = vec[3:4] + _dot(jnp.tanh(wa_in), w2_ref[...])
    w_log = -_softplus(-w_pre) - 0.5
    lw_o[...] = -jnp.exp(w_log)
    alpha = jax.nn.sigmoid(vec[4:5] + _dot(wa_in, a2_ref[...]))
    g_o[...] = _dot(jax.nn.sigmoid(lora[:, LANES:2 * LANES]), g2_ref[...])
    if has_vres:
        zv1 = _token_shift(zv1_ref[...], hv1_ref[...], first, muv_ref[...])
        mix = jax.nn.sigmoid(vec[7:8] + _dot(zv1, v2_ref[...]))
        v = v + (vf_ref[...] - v) * mix
    ones = _head_ones()
    kk = k * vec[5:6]
    kk = kk * lax.rsqrt(_head_sum(kk * kk, ones) + 1e-12)
    r_o[...] = r
    k_o[...] = k * (1.0 + (alpha - 1.0) * vec[6:7])
    v_o[...] = v
    a_o[...] = -kk
    b_o[...] = kk * alpha


def _rwkv_prep(z, zv1, v_first, vec, mu_l, w2p, a2p, g2, mu_v1, v2p):
    bsz, s, _ = z.shape
    tt = 256
    d = D_RWKV
    has_vres = zv1 is not None
    hstep = tt // SUBLANES

    def col(off, w):
        return lambda b, t: (b, t, off // w)

    def hcol(off, w):
        return lambda b, t: (b, jnp.maximum(t * hstep - 1, 0), off // w)

    lw = 2 * LANES
    in_specs = [pl.BlockSpec((None, tt, d), col(OFF_R, d)),
                pl.BlockSpec((None, tt, d), col(OFF_K, d)),
                pl.BlockSpec((None, tt, d), col(OFF_V, d)),
                pl.BlockSpec((None, tt, lw), col(OFF_LORA, lw)),
                pl.BlockSpec((None, SUBLANES, d), hcol(OFF_R, d)),
                pl.BlockSpec((None, SUBLANES, d), hcol(OFF_K, d)),
                pl.BlockSpec((None, SUBLANES, d), hcol(OFF_V, d)),
                pl.BlockSpec((None, SUBLANES, lw), hcol(OFF_LORA, lw)),
                pl.BlockSpec((SUBLANES, d), lambda b, t: (0, 0)),
                pl.BlockSpec((1, lw), lambda b, t: (0, 0)),
                pl.BlockSpec((LANES, d), lambda b, t: (0, 0)),
                pl.BlockSpec((LANES, d), lambda b, t: (0, 0)),
                pl.BlockSpec((LANES, d), lambda b, t: (0, 0))]
    args = [z, z, z, z, z, z, z, z, vec, mu_l, w2p, a2p, g2]
    if has_vres:
        in_specs += [pl.BlockSpec((None, tt, LANES), lambda b, t: (b, t, 0)),
                     pl.BlockSpec((None, SUBLANES, LANES),
                                  lambda b, t: (b, jnp.maximum(t * hstep - 1, 0), 0)),
                     pl.BlockSpec((None, tt, d), lambda b, t: (b, t, 0)),
                     pl.BlockSpec((1, LANES), lambda b, t: (0, 0)),
                     pl.BlockSpec((LANES, d), lambda b, t: (0, 0))]
        args += [zv1, zv1, v_first, mu_v1, v2p]
    out_spec = pl.BlockSpec((None, tt, d), lambda b, t: (b, t, 0))
    return pl.pallas_call(
        functools.partial(_rwkv_prep_kernel, has_vres=has_vres),
        grid=(bsz, s // tt),
        in_specs=in_specs,
        out_specs=[out_spec] * 7,
        out_shape=[jax.ShapeDtypeStruct((bsz, s, d), F32)] * 7,
        compiler_params=_cparams(("arbitrary", "arbitrary")),
        name="rwkv_prep",
    )(*args)


def _wkv_kernel(r_ref, k_ref, v_ref, a_ref, b_ref, lw_ref, g_ref, vec_ref, y_ref, s_scr):
    L = r_ref.shape[0]
    n_pairs = r_ref.shape[1] // LANES

    @pl.when(pl.program_id(1) == 0)
    def _():
        s_scr[...] = jnp.zeros_like(s_scr)

    lw = lw_ref[...]
    ti = lax.broadcasted_iota(jnp.int32, (L, L), 0)
    tj = lax.broadcasted_iota(jnp.int32, (L, L), 1)
    tri = jnp.where(ti >= tj, 1.0, 0.0).astype(BF16)
    l0, l1, l2 = _split3(lw)
    lc = (jnp.dot(tri, l0, preferred_element_type=F32) + jnp.dot(tri, l1, preferred_element_type=F32)
          + jnp.dot(tri, l2, preferred_element_type=F32))
    lc_last = lc[L - 1:L, :]
    r = r_ref[...]
    k = k_ref[...]
    v = v_ref[...]
    a = a_ref[...]
    b = b_ref[...]
    e_inv = jnp.exp(-lc)
    e_tail = jnp.exp(lc_last - lc)
    a_hat = a * jnp.exp(lc - lw)
    r_hat = r * jnp.exp(lc)
    b_hat = b * e_inv
    k_hat = k * e_inv
    b_til = b * e_tail
    k_til = k * e_tail
    d_last = jnp.exp(lc_last)

    lane = lax.broadcasted_iota(jnp.int32, (L, LANES), 1)
    even = lane < HEAD_DIM

    def cat(x):
        return jnp.concatenate([jnp.where(even, x, 0.0), jnp.where(even, 0.0, x)], axis=0)

    n2 = 2 * L
    pi = lax.broadcasted_iota(jnp.int32, (n2, n2), 0)
    qi = lax.broadcasted_iota(jnp.int32, (n2, n2), 1)
    same = (pi // L) == (qi // L)
    strict = same & ((pi % L) > (qi % L))
    incl = same & ((pi % L) >= (qi % L))
    eye = jnp.where(pi == qi, 1.0, 0.0)

    pairs = range(n_pairs)
    sls = [slice(p * LANES, (p + 1) * LANES) for p in pairs]
    ah_c = [cat(a_hat[:, sl]).astype(BF16) for sl in sls]
    rh_c = [cat(r_hat[:, sl]) for sl in sls]
    v_c = [cat(v[:, sl]).astype(BF16) for sl in sls]
    gram = [_dot_nt(jnp.concatenate([ah_c[p], rh_c[p].astype(BF16)], axis=0),
                    jnp.concatenate([cat(b_hat[:, sls[p]]), cat(k_hat[:, sls[p]])], axis=0))
            for p in pairs]
    nmat = [jnp.where(strict, gram[p][:n2, :n2], 0.0).astype(BF16) for p in pairs]
    a_ak = [jnp.where(strict, gram[p][:n2, n2:], 0.0) for p in pairs]
    a_rb = [jnp.where(incl, gram[p][n2:, :n2], 0.0) for p in pairs]
    a_rk = [jnp.where(incl, gram[p][n2:, n2:], 0.0) for p in pairs]
    x1 = [_dot(a_ak[p], v_c[p]) for p in pairs]
    ark_v = [_dot(a_rk[p], v_c[p]) for p in pairs]
    kt_v = [_dot_tn(cat(k_til[:, sls[p]]), v_c[p]) for p in pairs]
    tinv = [eye + nmat[p].astype(F32) for p in pairs]
    npow = nmat
    span = 2
    while span < L:
        npow = [_dot(npow[p], npow[p]).astype(BF16) for p in pairs]
        tinv = [tinv[p] + _dot(tinv[p], npow[p]) for p in pairs]
        span *= 2
    wu = [_dot(tinv[p], jnp.concatenate([ah_c[p], x1[p].astype(BF16)], axis=1)).astype(BF16)
          for p in pairs]
    qy = [_dot(a_rb[p], wu[p]) for p in pairs]
    mc = [_dot_tn(cat(b_til[:, sls[p]]), wu[p]) for p in pairs]
    s_prev = [s_scr[p].astype(BF16) for p in pairs]
    y_c = [_dot(rh_c[p] + qy[p][:, :LANES], s_prev[p]) + qy[p][:, LANES:] + ark_v[p] for p in pairs]
    for p in pairs:
        dl = jnp.broadcast_to(d_last[:, sls[p]], (n2, n2))
        m_mat = mc[p][:, :LANES] + jnp.where(pi == qi, dl, 0.0)
        s_scr[p] = _dot(m_mat, s_prev[p]) + mc[p][:, LANES:] + kt_v[p]

    y = jnp.concatenate([y_c[p][:L] + y_c[p][L:] for p in pairs], axis=1)
    vec = vec_ref[...]
    ones = _head_ones()
    inv_n = 1.0 / HEAD_DIM
    mu = _head_sum(y, ones) * inv_n
    dy = y - mu
    var = _head_sum(dy * dy, ones) * inv_n
    gn = dy * lax.rsqrt(var + RWKV_GN_EPS) * vec[1:2] + vec[2:3]
    bonus = _head_sum(r * k * vec[0:1], ones) * v
    y_ref[...] = ((gn + bonus) * g_ref[...]).astype(BF16)


def _wkv(r, k, v, a, b, lw, g, vec):
    bsz, s, d = r.shape
    L = WKV_CHUNK
    spec = pl.BlockSpec((None, L, d), lambda bb, c: (bb, c, 0))
    return pl.pallas_call(
        _wkv_kernel,
        grid=(bsz, s // L),
        in_specs=[spec] * 7 + [pl.BlockSpec((SUBLANES, d), lambda bb, c: (0, 0))],
        out_specs=spec,
        out_shape=jax.ShapeDtypeStruct((bsz, s, d), BF16),
        scratch_shapes=[pltpu.VMEM((d // LANES, LANES, LANES), F32)],
        compiler_params=_cparams(("arbitrary", "arbitrary")),
        name="wkv7",
    )(r, k, v, a, b, lw, g, vec)


def _merge_kernel(x_ref, ya_ref, yc_ref, o0_ref, o1_ref, o2_ref, l0_ref, l1_ref, l2_ref,
                  ga_ref, gb_ref, gc_ref, gt_ref, pa_ref, pb_ref, pc_ref, wo_ref, ln_ref, out_ref):
    l0 = l0_ref[...]
    l1 = l1_ref[...]
    l2 = l2_ref[...]
    lm = jnp.maximum(jnp.maximum(l0, l1), l2)
    e0 = jnp.exp(l0 - lm)
    e1 = jnp.exp(l1 - lm)
    e2 = jnp.exp(l2 - lm)
    yb = (e0 * o0_ref[...] + e1 * o1_ref[...] + e2 * o2_ref[...]) / (e0 + e1 + e2)
    merged = (jax.nn.sigmoid(ga_ref[...]) * jnp.dot(ya_ref[...], pa_ref[...], preferred_element_type=F32)
              + jax.nn.sigmoid(gb_ref[...]) * _dot(yb, pb_ref[...])
              + jax.nn.sigmoid(gc_ref[...]) * jnp.dot(yc_ref[...], pc_ref[...], preferred_element_type=F32))
    y = _dot(merged, wo_ref[...])
    ln = ln_ref[...]
    out_ref[...] = _layer_norm(ALPHA * x_ref[...] + (1.0 + gt_ref[...]) * y, ln[0:1], ln[1:2])


def _merge(x, z, ya, yc, att, gt, pa, pb, pc, wo, ln):
    bsz, s, d = x.shape
    tm = 256
    wa = D_ATTN_OUT
    row = lambda w, off=0: pl.BlockSpec((None, tm, w), lambda b, i: (b, i, off // w))
    const = lambda shp: pl.BlockSpec(shp, lambda b, i: (0,) * len(shp))
    (o0, l0), (o1, l1), (o2, l2) = att
    return pl.pallas_call(
        _merge_kernel,
        grid=(bsz, s // tm),
        in_specs=[row(d), row(d), row(d)] + [row(wa)] * 6
                 + [row(d, OFF_GATE), row(d, OFF_GATE + d), row(d, OFF_GATE + 2 * d),
                    pl.BlockSpec((None, 1, d), lambda b, i: (b, 0, 0)),
                    const((d, d)), const((wa, d)), const((d, d)), const((d, d)),
                    const((SUBLANES, d))],
        out_specs=row(d),
        out_shape=jax.ShapeDtypeStruct((bsz, s, d), F32),
        compiler_params=_cparams(("arbitrary", "arbitrary")),
        name="merge_proj_ln",
    )(x, ya, yc, o0, o1, o2, l0, l1, l2, z, z, z, gt, pa, pb, pc, wo, ln)


def _ffn_kernel(x_ref, halo_ref, sc_ref, sh_ref, gt_ref, wg_ref, wv_ref, cwg_ref, cwv_ref,
                wd_ref, ln_ref, out_ref, h_scr, ug_scr, uv_scr, acc):
    i = pl.program_id(1)
    f = pl.program_id(2)
    tm = x_ref.shape[0]
    hr = halo_ref.shape[0]

    @pl.when(f == 0)
    def _():
        sc = 1.0 + sc_ref[...]
        sh = sh_ref[...]
        hh = jnp.where(i == 0, 0.0, halo_ref[...] * sc + sh)
        h_scr[0:hr, :] = hh.astype(BF16)
        h_scr[hr:hr + tm, :] = (x_ref[...] * sc + sh).astype(BF16)

    h = h_scr[...]
    ug_scr[...] = jnp.dot(h, wg_ref[...], preferred_element_type=F32)
    uv_scr[...] = jnp.dot(h, wv_ref[...], preferred_element_type=F32)

    def conv(u_scr, cw):
        out = cw[CONV_F:CONV_F + 1]
        for j in range(CONV_F):
            out = out + cw[j:j + 1] * u_scr[hr - j:hr - j + tm, :]
        return out

    cg = conv(ug_scr, cwg_ref[...])
    cv = conv(uv_scr, cwv_ref[...])
    act = cg * jax.nn.sigmoid(cg) * cv
    part = _dot(act, wd_ref[...])

    @pl.when(f == 0)
    def _():
        acc[...] = part

    @pl.when(f > 0)
    def _():
        acc[...] += part

    @pl.when(f == pl.num_programs(2) - 1)
    def _():
        ln = ln_ref[...]
        out_ref[...] = _layer_norm(ALPHA * x_ref[...] + (1.0 + gt_ref[...]) * acc[...],
                                   ln[0:1], ln[1:2])


def _conv_ffn(x, sc, sh, gt, w_up, conv_wb, w_down, ln):
    bsz, s, d = x.shape
    tm, fc = 512, 1408
    nf = D_FF // fc
    hr = BF16_ROWS
    mod = pl.BlockSpec((None, 1, d), lambda b, i, f: (b, 0, 0))
    return pl.pallas_call(
        _ffn_kernel,
        grid=(bsz, s // tm, nf),
        in_specs=[pl.BlockSpec((None, tm, d), lambda b, i, f: (b, i, 0)),
                  pl.BlockSpec((None, hr, d), lambda b, i, f: (b, jnp.maximum(i * (tm // hr) - 1, 0), 0)),
                  mod, mod, mod,
                  pl.BlockSpec((d, fc), lambda b, i, f: (0, f)),
                  pl.BlockSpec((d, fc), lambda b, i, f: (0, nf + f)),
                  pl.BlockSpec((SUBLANES, fc), lambda b, i, f: (0, f)),
                  pl.BlockSpec((SUBLANES, fc), lambda b, i, f: (0, nf + f)),
                  pl.BlockSpec((fc, d), lambda b, i, f: (f, 0)),
                  pl.BlockSpec((SUBLANES, d), lambda b, i, f: (0, 0))],
        out_specs=pl.BlockSpec((None, tm, d), lambda b, i, f: (b, i, 0)),
        out_shape=jax.ShapeDtypeStruct((bsz, s, d), F32),
        scratch_shapes=[pltpu.VMEM((tm + hr, d), BF16),
                        pltpu.VMEM((tm + hr, fc), F32),
                        pltpu.VMEM((tm + hr, fc), F32),
                        pltpu.VMEM((tm, d), F32)],
        compiler_params=_cparams(("arbitrary", "arbitrary", "arbitrary")),
        name="conv_ffn_ln",
    )(x, x, sc, sh, gt, w_up, w_up, conv_wb, conv_wb, w_down, ln)


def _rows(vectors, n_rows=SUBLANES):
    m = jnp.stack([v.astype(F32) for v in vectors], axis=0)
    return jnp.pad(m, ((0, n_rows - m.shape[0]), (0, 0)))


def _block_diag(w, per):
    g, n, _ = w.shape
    w = w.reshape(g // per, per, n, n)
    eye = jnp.eye(per, dtype=w.dtype)
    out = jnp.einsum('cpij,pq->cpiqj', w, eye)
    return out.reshape(g // per, per * n, per * n)


def kernel(x, c, positions, mod_w, mod_b, w_in, w_in_vres, conv_a_w, conv_a_b, lru_wa, lru_ba, lru_wx, lru_bx, lru_lambda, rwkv_mu, mu_vres, w0, w2, a0, a2, g2, v0, v2, k_k, k_a, r_k, ln_x_w, ln_x_b, proj_a, proj_b, proj_c, w_o, ln1_w, ln1_b, ffn_up, ffn_conv_w, ffn_conv_b, ffn_down, ln2_w, ln2_b):
    bsz, s, d = x.shape
    mod = _modulation(c, mod_w, mod_b)
    half = ROT_DIM // 2
    inv_freq = ROPE_THETA ** (-jnp.arange(half, dtype=F32) / half)
    lane = jnp.arange(LANES)
    freq_lane = jnp.where((lane % HEAD_DIM) < ROT_DIM, inv_freq[lane % half], 0.0).reshape(1, LANES)
    pos3 = positions.reshape(bsz, s, 1)
    zero_d = jnp.zeros((d,), F32)

    v_first = None
    for l in range(DEPTH):
        m6 = mod[l].reshape(bsz, 6, 1, d)
        sh1, sc1, gt1, sh2, sc2, gt2 = (m6[:, i] for i in range(6))

        wl = w_in[l]
        w_perm = jnp.concatenate([wl[:, 0:2048], wl[:, 4352:7424], wl[:, 7680:10752],
                                  wl[:, 2048:4352], wl[:, 7424:7680]], axis=1).astype(BF16)
        if l == 0:
            w_vres = None
        else:
            w_vres = jnp.pad(w_in_vres[l - 1], ((0, 0), (0, LANES - MV_LORA))).astype(BF16)
        z, zv1 = _in_projection(x, sc1, sh1, w_perm, w_vres)

        per = 256 // RNN_BLOCK
        y_a = _rglru(z, conv_a_w[l],
                     _rows([conv_a_b[l], lru_ba[l], lru_bx[l], lru_lambda[l]]),
                     _block_diag(lru_wa[l], per).astype(BF16),
                     _block_diag(lru_wx[l], per).astype(BF16))

        qkv = _rope(z, pos3, freq_lane)
        att = [_attention_group(qkv, g, dil) for g, (_, dil) in enumerate(ATTN_GROUPS)]

        mu = rwkv_mu[l]
        vec = _rows([mu[0:1024], mu[1024:2048], mu[2048:3072], w0[l], a0[l], k_k[l], k_a[l],
                     v0[l - 1] if l > 0 else zero_d])
        mu_l = mu[3072:3328].reshape(1, 2 * LANES)
        w2p = jnp.pad(w2[l], ((0, LANES - LORA_W), (0, 0))).astype(BF16)
        a2p = jnp.pad(a2[l], ((LORA_W, 0), (0, 0))).astype(BF16)
        if l == 0:
            prep = _rwkv_prep(z, None, None, vec, mu_l, w2p, a2p, g2[l].astype(BF16), None, None)
        else:
            mu_v1 = jnp.pad(mu_vres[l - 1], (0, LANES - MV_LORA)).reshape(1, LANES)
            v2p = jnp.pad(v2[l - 1], ((0, LANES - MV_LORA), (0, 0))).astype(BF16)
            prep = _rwkv_prep(z, zv1, v_first, vec, mu_l, w2p, a2p, g2[l].astype(BF16), mu_v1, v2p)
        r_, k_, v_, a_, b_, lw_, g_ = prep
        if l == 0:
            v_first = v_
        y_c = _wkv(r_, k_, v_, a_, b_, lw_, g_,
                   _rows([r_k[l].reshape(-1), ln_x_w[l], ln_x_b[l]]))

        x = _merge(x, z, y_a, y_c, att, gt1, proj_a[l].astype(BF16), proj_b[l].astype(BF16),
                   proj_c[l].astype(BF16), w_o[l].astype(BF16), _rows([ln1_w[l], ln1_b[l]]))

        conv_wb = jnp.concatenate([ffn_conv_w[l], ffn_conv_b[l][None, :]], axis=0)
        conv_wb = jnp.pad(conv_wb, ((0, SUBLANES - conv_wb.shape[0]), (0, 0)))
        x = _conv_ffn(x, sc2, sh2, gt2, ffn_up[l].astype(BF16), conv_wb,
                      ffn_down[l].astype(BF16), _rows([ln2_w[l], ln2_b[l]]))
    return x
```

```python
import functools

import jax
import jax.numpy as jnp
from jax import lax
from jax.experimental import pallas as pl
from jax.experimental.pallas import tpu as pltpu

F32 = jnp.float32
BF16 = jnp.bfloat16

D_MODEL = 1024
DEPTH = 2
D_RNN = 1024
RNN_BLOCK = 64
CONV_A = 4
LRU_C = 8.0
HEAD_DIM = 64
ATTN_GROUPS = ((128, 1), (512, 4), (2048, 16))
HEADS_PER_GROUP = 4
N_GROUPS = 3
D_ATTN = 768
D_ATTN_OUT = 256
ROT_DIM = 16
ROPE_THETA = 500000.0
D_RWKV = 1024
LORA_W = 64
LORA_A = 64
LORA_G = 128
MV_LORA = 32
RWKV_GN_EPS = 64e-5
D_FF = 2816
CONV_F = 3
ALPHA = (2 * DEPTH) ** 0.25
LN_EPS = 1e-5

LANES = 128
SUBLANES = 8
BF16_ROWS = 16

OFF_XA = 0
OFF_GA = 1024
OFF_R = 2048
OFF_K = 3072
OFF_V = 4096
OFF_GATE = 5120
OFF_LORA = 8192
IN_TILE = 768
N_MAIN_TILES = 11

WKV_CHUNK = 64
ATTN_BLOCK = 128
VMEM_LIMIT = 52 * 1024 * 1024


def _cparams(sem):
    return pltpu.CompilerParams(dimension_semantics=sem, vmem_limit_bytes=VMEM_LIMIT)


def _dot(a, b):
    return jnp.dot(a.astype(BF16), b.astype(BF16), preferred_element_type=F32)


def _dot_nt(a, b):
    return lax.dot_general(a.astype(BF16), b.astype(BF16), (((1,), (1,)), ((), ())),
                           preferred_element_type=F32)


def _dot_tn(a, b):
    return lax.dot_general(a.astype(BF16), b.astype(BF16), (((0,), (0,)), ((), ())),
                           preferred_element_type=F32)


def _split3(x):
    p0 = x.astype(BF16)
    r1 = x - p0.astype(F32)
    p1 = r1.astype(BF16)
    p2 = (r1 - p1.astype(F32)).astype(BF16)
    return p0, p1, p2


def _head_ones():
    i = lax.broadcasted_iota(jnp.int32, (LANES, LANES), 0) // HEAD_DIM
    j = lax.broadcasted_iota(jnp.int32, (LANES, LANES), 1) // HEAD_DIM
    return jnp.where(i == j, 1.0, 0.0).astype(BF16)


def _head_sum(x, ones):
    rows = x.shape[0]
    n = x.shape[1] // LANES
    xs = jnp.concatenate([x[:, c * LANES:(c + 1) * LANES] for c in range(n)], axis=0)
    hi = xs.astype(BF16)
    lo = (xs - hi.astype(F32)).astype(BF16)
    out = jnp.dot(jnp.concatenate([hi, lo], axis=0), ones, preferred_element_type=F32)
    tot = out[:n * rows] + out[n * rows:]
    return jnp.concatenate([tot[c * rows:(c + 1) * rows] for c in range(n)], axis=1)


def _softplus(x):
    return jnp.maximum(x, 0.0) + jnp.log1p(jnp.exp(-jnp.abs(x)))


def _layer_norm(x, w, b):
    mu = jnp.mean(x, axis=-1, keepdims=True)
    d = x - mu
    var = jnp.mean(d * d, axis=-1, keepdims=True)
    return d * lax.rsqrt(var + LN_EPS) * w + b


def _mod_kernel(c_ref, w_ref, b_ref, o_ref):
    c = c_ref[...]
    s = c * jax.nn.sigmoid(c)
    s0, s1, s2 = _split3(s)
    w0, w1, w2 = _split3(w_ref[...])
    acc = jnp.dot(s0, w0, preferred_element_type=F32)
    acc += jnp.dot(s0, w1, preferred_element_type=F32) + jnp.dot(s1, w0, preferred_element_type=F32)
    acc += (jnp.dot(s1, w1, preferred_element_type=F32) + jnp.dot(s0, w2, preferred_element_type=F32)
            + jnp.dot(s2, w0, preferred_element_type=F32))
    o_ref[...] = acc + b_ref[...]


def _modulation(c, mod_w, mod_b):
    depth, d, n = mod_w.shape
    bsz = c.shape[0]
    rows = BF16_ROWS
    tn = 1536
    c_pad = jnp.pad(c, ((0, rows - bsz), (0, 0)))
    out = pl.pallas_call(
        _mod_kernel,
        grid=(depth, n // tn),
        in_specs=[pl.BlockSpec((rows, d), lambda l, j: (0, 0)),
                  pl.BlockSpec((None, d, tn), lambda l, j: (l, 0, j)),
                  pl.BlockSpec((None, 1, tn), lambda l, j: (l, 0, j))],
        out_specs=pl.BlockSpec((None, rows, tn), lambda l, j: (l, 0, j)),
        out_shape=jax.ShapeDtypeStruct((depth, rows, n), F32),
        compiler_params=_cparams(("arbitrary", "arbitrary")),
        name="adaln_mod",
    )(c_pad, mod_w, mod_b.reshape(depth, 1, n))
    return out[:, :bsz]


def _inproj_kernel(x_ref, sc_ref, sh_ref, w_ref, cos_ref, sa_ref, sb_ref, *rest, has_vres):
    if has_vres:
        wv_ref, z_ref, g0_ref, g1_ref, g2_ref, zv_ref, h_scr, r_scr = rest
    else:
        z_ref, g0_ref, g1_ref, g2_ref, h_scr, r_scr = rest
    j = pl.program_id(2)
    tm = x_ref.shape[0]
    width = HEADS_PER_GROUP * HEAD_DIM
    half = ROT_DIM // 2

    @pl.when(j == 0)
    def _():
        h_scr[...] = (x_ref[...] * (1.0 + sc_ref[...]) + sh_ref[...]).astype(BF16)
        if has_vres:
            zv_ref[...] = jnp.dot(h_scr[...], wv_ref[...], preferred_element_type=F32)

    acc = jnp.dot(h_scr[...], w_ref[...], preferred_element_type=F32)

    @pl.when(j < N_MAIN_TILES)
    def _():
        z_ref[...] = acc

    for sec in range(3):
        @pl.when(j == N_MAIN_TILES + sec)
        def _(sec=sec):
            if sec < 2:
                scale = HEAD_DIM ** -0.5 if sec == 0 else 1.0
                for c in range(acc.shape[1] // LANES):
                    sl = slice(c * LANES, (c + 1) * LANES)
                    t = acc[:, sl]
                    rot = (t * cos_ref[...] + pltpu.roll(t, half, 1) * sa_ref[...]
                           + pltpu.roll(t, LANES - half, 1) * sb_ref[...])
                    r_scr[c] = rot * scale
            else:
                for c in range(acc.shape[1] // LANES):
                    r_scr[c] = acc[:, c * LANES:(c + 1) * LANES]
            per = width // LANES
            for g, g_ref in enumerate((g0_ref, g1_ref, g2_ref)):
                dil = g_ref.shape[0]
                for r in range(dil):
                    rows = pl.ds(r, tm // dil, stride=dil) if dil > 1 else slice(None)
                    for c in range(per):
                        lo = sec * width + c * LANES
                        g_ref[r, :, lo:lo + LANES] = r_scr[g * per + c, rows, :].astype(BF16)


def _in_projection(x, sc, sh, w, w_vres, rope_tabs):
    bsz, s, d = x.shape
    n = w.shape[1]
    tm, tn = 1024, IN_TILE
    has_vres = w_vres is not None
    tab = pl.BlockSpec((None, tm, LANES), lambda b, i, j: (b, i, 0))
    in_specs = [pl.BlockSpec((None, tm, d), lambda b, i, j: (b, i, 0)),
                pl.BlockSpec((None, 1, d), lambda b, i, j: (b, 0, 0)),
                pl.BlockSpec((None, 1, d), lambda b, i, j: (b, 0, 0)),
                pl.BlockSpec((d, tn), lambda b, i, j: (0, j)),
                tab, tab, tab]
    out_specs = [pl.BlockSpec((None, tm, tn), lambda b, i, j: (b, i, jnp.minimum(j, N_MAIN_TILES - 1)))]
    out_shape = [jax.ShapeDtypeStruct((bsz, s, N_MAIN_TILES * tn), F32)]
    for _, dil in ATTN_GROUPS:
        out_specs.append(pl.BlockSpec((None, dil, tm // dil, tn), lambda b, i, j: (b, 0, i, 0)))
        out_shape.append(jax.ShapeDtypeStruct((bsz, dil, s // dil, tn), BF16))
    args = [x, sc, sh, w, *rope_tabs]
    if has_vres:
        in_specs.append(pl.BlockSpec((d, LANES), lambda b, i, j: (0, 0)))
        out_specs.append(pl.BlockSpec((None, tm, LANES), lambda b, i, j: (b, i, 0)))
        out_shape.append(jax.ShapeDtypeStruct((bsz, s, LANES), F32))
        args.append(w_vres)
    outs = pl.pallas_call(
        functools.partial(_inproj_kernel, has_vres=has_vres),
        grid=(bsz, s // tm, n // tn),
        in_specs=in_specs,
        out_specs=out_specs,
        out_shape=out_shape,
        scratch_shapes=[pltpu.VMEM((tm, d), BF16), pltpu.VMEM((tn // LANES, tm, LANES), F32)],
        compiler_params=_cparams(("arbitrary", "arbitrary", "arbitrary")),
        name="in_projection",
    )(*args)
    return outs[0], outs[1:4], (outs[4] if has_vres else None)


def _rglru_kernel(xa_ref, ga_ref, cw_ref, vec_ref, wa_ref, wx_ref, y_ref, xbuf, hc):
    t = pl.program_id(2)
    tt, cwid = xa_ref.shape

    @pl.when(t == 0)
    def _():
        xbuf[0:SUBLANES, :] = jnp.zeros((SUBLANES, cwid), F32)
        hc[...] = jnp.zeros_like(hc)

    xa = xa_ref[...]
    xbuf[SUBLANES:SUBLANES + tt, :] = xa
    cw = cw_ref[...]
    vec = vec_ref[...]
    xc = vec[0:1] + cw[0:1] * xa
    for j in range(1, CONV_A):
        xc = xc + cw[j:j + 1] * xbuf[SUBLANES - j:SUBLANES - j + tt, :]
    xbuf[0:SUBLANES, :] = xbuf[tt:tt + SUBLANES, :]

    xb = xc.astype(BF16)
    r = jax.nn.sigmoid(jnp.dot(xb, wa_ref[...], preferred_element_type=F32) + vec[1:2])
    i = jax.nn.sigmoid(jnp.dot(xb, wx_ref[...], preferred_element_type=F32) + vec[2:3])
    log_a = (-LRU_C) * r * _softplus(-vec[3:4])
    a = jnp.exp(log_a)
    b = jnp.sqrt(-jnp.tanh(log_a) * (a * a + 1.0)) * (i * xc)

    ng = tt // SUBLANES
    a = a.reshape(ng, SUBLANES, cwid)
    b = b.reshape(ng, SUBLANES, cwid)
    rows = lax.broadcasted_iota(jnp.int32, (ng, SUBLANES, cwid), 1)
    step = 1
    while step < SUBLANES:
        a_sh = pltpu.roll(a, step, 1)
        b_sh = pltpu.roll(b, step, 1)
        m = rows >= step
        b = jnp.where(m, a * b_sh + b, b)
        a = jnp.where(m, a * a_sh, a)
        step *= 2
    g = ga_ref[...]
    gelu = 0.5 * g * (1.0 + jnp.tanh(0.7978845608028654 * (g + 0.044715 * (g * g * g))))
    carry = hc[0:1, :]
    hs = []
    for i in range(ng):
        h = a[i] * carry + b[i]
        carry = h[SUBLANES - 1:SUBLANES, :]
        hs.append(h)
    hc[0:1, :] = carry
    y_ref[...] = (jnp.concatenate(hs, axis=0) * gelu).astype(BF16)


def _rglru(z, conv_w, vec, wa_bd, wx_bd):
    bsz, s, _ = z.shape
    tt, cwid = 256, 256
    nc = D_RNN // cwid
    return pl.pallas_call(
        _rglru_kernel,
        grid=(bsz, nc, s // tt),
        in_specs=[pl.BlockSpec((None, tt, cwid), lambda b, c, t: (b, t, OFF_XA // cwid + c)),
                  pl.BlockSpec((None, tt, cwid), lambda b, c, t: (b, t, OFF_GA // cwid + c)),
                  pl.BlockSpec((CONV_A, cwid), lambda b, c, t: (0, c)),
                  pl.BlockSpec((SUBLANES, cwid), lambda b, c, t: (0, c)),
                  pl.BlockSpec((None, cwid, cwid), lambda b, c, t: (c, 0, 0)),
                  pl.BlockSpec((None, cwid, cwid), lambda b, c, t: (c, 0, 0))],
        out_specs=pl.BlockSpec((None, tt, cwid), lambda b, c, t: (b, t, c)),
        out_shape=jax.ShapeDtypeStruct((bsz, s, D_RNN), BF16),
        scratch_shapes=[pltpu.VMEM((tt + SUBLANES, cwid), F32), pltpu.VMEM((SUBLANES, cwid), F32)],
        compiler_params=_cparams(("arbitrary", "arbitrary", "arbitrary")),
        name="rglru",
    )(z, z, conv_w, vec, wa_bd, wx_bd)


def _rope_table_kernel(pos_ref, freq_ref, cos_ref, sa_ref, sb_ref):
    tt = pos_ref.shape[0]
    ang = pos_ref[...].astype(F32) * freq_ref[...]
    lane = lax.broadcasted_iota(jnp.int32, (tt, LANES), 1) % HEAD_DIM
    sn = jnp.sin(ang)
    half = ROT_DIM // 2
    cos_ref[...] = jnp.cos(ang)
    sa_ref[...] = jnp.where((lane >= half) & (lane < ROT_DIM), sn, 0.0)
    sb_ref[...] = jnp.where(lane < half, -sn, 0.0)


def _rope_tables(pos3, freq_lane):
    bsz, s, _ = pos3.shape
    tt = 1024
    out = pl.BlockSpec((None, tt, LANES), lambda b, t: (b, t, 0))
    return pl.pallas_call(
        _rope_table_kernel,
        grid=(bsz, s // tt),
        in_specs=[pl.BlockSpec((None, tt, 1), lambda b, t: (b, t, 0)),
                  pl.BlockSpec((1, LANES), lambda b, t: (0, 0))],
        out_specs=[out] * 3,
        out_shape=[jax.ShapeDtypeStruct((bsz, s, LANES), F32)] * 3,
        compiler_params=_cparams(("arbitrary", "arbitrary")),
        name="rope_tables",
    )(pos3, freq_lane)


def _attn_kernel(q_ref, kp_ref, kc_ref, vp_ref, vc_ref, o_ref, lse_ref):
    mb = pl.program_id(2)
    nq, width = q_ref.shape
    nh = HEADS_PER_GROUP
    q = q_ref[...]
    k = jnp.concatenate([kp_ref[...], kc_ref[...]], axis=0)
    v = jnp.concatenate([vp_ref[...], vc_ref[...]], axis=0)
    nk = 2 * nq
    head = lax.broadcasted_iota(jnp.int32, (nq, width), 1) // HEAD_DIM
    qcat = jnp.concatenate([jnp.where(head == h, q, jnp.zeros_like(q)) for h in range(nh)], axis=0)
    s = lax.dot_general(qcat, k, (((1,), (1,)), ((), ())), preferred_element_type=F32)
    qi = lax.broadcasted_iota(jnp.int32, (nh * nq, nk), 0) % nq
    kj = lax.broadcasted_iota(jnp.int32, (nh * nq, nk), 1)
    valid = (kj >= qi) & (kj <= qi + nq) & ((mb > 0) | (kj >= nq))
    s = jnp.where(valid, s, -1e30)
    m = jnp.max(s, axis=1, keepdims=True)
    p = jnp.exp(s - m)
    l = jnp.sum(p, axis=1, keepdims=True)
    pv = jnp.dot(p.astype(BF16), v, preferred_element_type=F32) / l
    lse_rows = m + jnp.log(l)
    o = jnp.zeros((nq, width), F32)
    lse = jnp.zeros((nq, width), F32)
    for h in range(nh):
        hm = head == h
        rows = slice(h * nq, (h + 1) * nq)
        o = jnp.where(hm, pv[rows], o)
        lse = jnp.where(hm, lse_rows[rows], lse)
    o_ref[...] = o
    lse_ref[...] = lse


def _attention_group(qkv, g):
    bsz, dil, sub, _ = qkv.shape
    width = HEADS_PER_GROUP * HEAD_DIM
    nb = ATTN_BLOCK

    def cur(sec):
        return lambda b, r, m: (b, r, m, sec)

    def prev(sec):
        return lambda b, r, m: (b, r, jnp.maximum(m - 1, 0), sec)

    blk = (None, None, nb, width)
    return pl.pallas_call(
        _attn_kernel,
        grid=(bsz, dil, sub // nb),
        in_specs=[pl.BlockSpec(blk, cur(0)),
                  pl.BlockSpec(blk, prev(1)), pl.BlockSpec(blk, cur(1)),
                  pl.BlockSpec(blk, prev(2)), pl.BlockSpec(blk, cur(2))],
        out_specs=[pl.BlockSpec(blk, cur(0)), pl.BlockSpec(blk, cur(0))],
        out_shape=[jax.ShapeDtypeStruct((bsz, dil, sub, width), F32)] * 2,
        compiler_params=_cparams(("arbitrary", "arbitrary", "arbitrary")),
        name=f"dilated_attention_g{g}",
    )(qkv, qkv, qkv, qkv, qkv)


def _token_shift(z, halo, first, mu):
    prev = jnp.where(first, 0.0, halo[SUBLANES - 1:SUBLANES, :])
    zp = pltpu.roll(z, 1, 0)
    rows = lax.broadcasted_iota(jnp.int32, z.shape, 0)
    zp = jnp.where(rows == 0, prev, zp)
    return z + (zp - z) * mu


def _rwkv_prep_kernel(*refs, has_vres):
    if has_vres:
        (zr_ref, zk_ref, zv_ref, zl_ref, hr_ref, hk_ref, hv_ref, hl_ref, vec_ref, mul_ref,
         w2_ref, a2_ref, g2_ref, zv1_ref, hv1_ref, vf_ref, muv_ref, v2_ref,
         r_o, k_o, v_o, a_o, b_o, lw_o, g_o) = refs
    else:
        (zr_ref, zk_ref, zv_ref, zl_ref, hr_ref, hk_ref, hv_ref, hl_ref, vec_ref, mul_ref,
         w2_ref, a2_ref, g2_ref,
         r_o, k_o, v_o, a_o, b_o, lw_o, g_o) = refs
    first = pl.program_id(1) == 0
    vec = vec_ref[...]
    r = _token_shift(zr_ref[...], hr_ref[...], first, vec[0:1])
    k = _token_shift(zk_ref[...], hk_ref[...], first, vec[1:2])
    v = _token_shift(zv_ref[...], hv_ref[...], first, vec[2:3])
    lora = _token_shift(zl_ref[...], hl_ref[...], first, mul_ref[...])
    wa_in = lora[:, 0:LANES]
    w_pre = vec[3:4] + _dot(jnp.tanh(wa_in), w2_ref[...])
    w_log = -_softplus(-w_pre) - 0.5
    lw_o[...] = -jnp.exp(w_log)
    alpha = jax.nn.sigmoid(vec[4:5] + _dot(wa_in, a2_ref[...]))
    g_o[...] = _dot(jax.nn.sigmoid(lora[:, LANES:2 * LANES]), g2_ref[...])
    if has_vres:
        zv1 = _token_shift(zv1_ref[...], hv1_ref[...], first, muv_ref[...])
        mix = jax.nn.sigmoid(vec[7:8] + _dot(zv1, v2_ref[...]))
        v = v + (vf_ref[...] - v) * mix
    ones = _head_ones()
    kk = k * vec[5:6]
    kk = kk * lax.rsqrt(_head_sum(kk * kk, ones) + 1e-12)
    r_o[...] = r
    k_o[...] = k * (1.0 + (alpha - 1.0) * vec[6:7])
    v_o[...] = v
    a_o[...] = -kk
    b_o[...] = kk * alpha


def _rwkv_prep(z, zv1, v_first, vec, mu_l, w2p, a2p, g2, mu_v1, v2p):
    bsz, s, _ = z.shape
    tt = 256
    d = D_RWKV
    has_vres = zv1 is not None
    hstep = tt // SUBLANES

    def col(off, w):
        return lambda b, t: (b, t, off // w)

    def hcol(off, w):
        return lambda b, t: (b, jnp.maximum(t * hstep - 1, 0), off // w)

    lw = 2 * LANES
    in_specs = [pl.BlockSpec((None, tt, d), col(OFF_R, d)),
                pl.BlockSpec((None, tt, d), col(OFF_K, d)),
                pl.BlockSpec((None, tt, d), col(OFF_V, d)),
                pl.BlockSpec((None, tt, lw), col(OFF_LORA, lw)),
                pl.BlockSpec((None, SUBLANES, d), hcol(OFF_R, d)),
                pl.BlockSpec((None, SUBLANES, d), hcol(OFF_K, d)),
                pl.BlockSpec((None, SUBLANES, d), hcol(OFF_V, d)),
                pl.BlockSpec((None, SUBLANES, lw), hcol(OFF_LORA, lw)),
                pl.BlockSpec((SUBLANES, d), lambda b, t: (0, 0)),
                pl.BlockSpec((1, lw), lambda b, t: (0, 0)),
                pl.BlockSpec((LANES, d), lambda b, t: (0, 0)),
                pl.BlockSpec((LANES, d), lambda b, t: (0, 0)),
                pl.BlockSpec((LANES, d), lambda b, t: (0, 0))]
    args = [z, z, z, z, z, z, z, z, vec, mu_l, w2p, a2p, g2]
    if has_vres:
        in_specs += [pl.BlockSpec((None, tt, LANES), lambda b, t: (b, t, 0)),
                     pl.BlockSpec((None, SUBLANES, LANES),
                                  lambda b, t: (b, jnp.maximum(t * hstep - 1, 0), 0)),
                     pl.BlockSpec((None, tt, d), lambda b, t: (b, t, 0)),
                     pl.BlockSpec((1, LANES), lambda b, t: (0, 0)),
                     pl.BlockSpec((LANES, d), lambda b, t: (0, 0))]
        args += [zv1, zv1, v_first, mu_v1, v2p]
    out_spec = pl.BlockSpec((None, tt, d), lambda b, t: (b, t, 0))
    return pl.pallas_call(
        functools.partial(_rwkv_prep_kernel, has_vres=has_vres),
        grid=(bsz, s // tt),
        in_specs=in_specs,
        out_specs=[out_spec] * 7,
        out_shape=[jax.ShapeDtypeStruct((bsz, s, d), F32)] * 7,
        compiler_params=_cparams(("arbitrary", "arbitrary")),
        name="rwkv_prep",
    )(*args)


def _wkv_kernel(r_ref, k_ref, v_ref, a_ref, b_ref, lw_ref, g_ref, vec_ref, y_ref, s_scr):
    L = r_ref.shape[0]
    n_pairs = r_ref.shape[1] // LANES

    @pl.when(pl.program_id(1) == 0)
    def _():
        s_scr[...] = jnp.zeros_like(s_scr)

    lw = lw_ref[...]
    ti = lax.broadcasted_iota(jnp.int32, (L, L), 0)
    tj = lax.broadcasted_iota(jnp.int32, (L, L), 1)
    tri = jnp.where(ti >= tj, 1.0, 0.0).astype(BF16)
    l0, l1, l2 = _split3(lw)
    lc = (jnp.dot(tri, l0, preferred_element_type=F32) + jnp.dot(tri, l1, preferred_element_type=F32)
          + jnp.dot(tri, l2, preferred_element_type=F32))
    lc_last = lc[L - 1:L, :]
    r = r_ref[...]
    k = k_ref[...]
    v = v_ref[...]
    a = a_ref[...]
    b = b_ref[...]
    e_inv = jnp.exp(-lc)
    e_tail = jnp.exp(lc_last - lc)
    a_hat = a * jnp.exp(lc - lw)
    r_hat = r * jnp.exp(lc)
    b_hat = b * e_inv
    k_hat = k * e_inv
    b_til = b * e_tail
    k_til = k * e_tail
    d_last = jnp.exp(lc_last)

    even = lax.broadcasted_iota(jnp.int32, (L, LANES), 1) < HEAD_DIM

    def cat(x):
        x = x.astype(BF16)
        zero = jnp.zeros_like(x)
        return jnp.concatenate([jnp.where(even, x, zero), jnp.where(even, zero, x)], axis=0)

    def stack(*xs):
        return jnp.concatenate([x.astype(BF16) for x in xs], axis=0)

    ti = lax.broadcasted_iota(jnp.int32, (L, 2 * L), 0)
    tj = lax.broadcasted_iota(jnp.int32, (L, 2 * L), 1) % L
    strict = ti > tj
    incl = ti >= tj
    eye = jnp.where(ti == tj, 1.0, 0.0)
    pi = lax.broadcasted_iota(jnp.int32, (LANES, LANES), 0)
    qi = lax.broadcasted_iota(jnp.int32, (LANES, LANES), 1)
    same_head = (pi // HEAD_DIM) == (qi // HEAD_DIM)

    pairs = range(n_pairs)
    sls = [slice(p * LANES, (p + 1) * LANES) for p in pairs]
    v_c = [cat(v[:, sl]) for sl in sls]
    gram = [_dot_nt(stack(a_hat[:, sl], r_hat[:, sl]),
                    jnp.concatenate([cat(b_hat[:, sl]), cat(k_hat[:, sl])], axis=0))
            for sl in sls]
    nmat = [jnp.where(strict, gram[p][:L, :LANES], 0.0) for p in pairs]
    a_ak = [jnp.where(strict, gram[p][:L, LANES:], 0.0) for p in pairs]
    a_rb = [jnp.where(incl, gram[p][L:, :LANES], 0.0) for p in pairs]
    a_rk = [jnp.where(incl, gram[p][L:, LANES:], 0.0) for p in pairs]
    av = [_dot(stack(a_ak[p], a_rk[p]), v_c[p]) for p in pairs]
    tinv = [eye + nmat[p] for p in pairs]
    npow = [_dot(nmat[p], cat(nmat[p])) for p in pairs]
    span = 4
    while span < L:
        prod = [_dot(stack(tinv[p], npow[p]), cat(npow[p])) for p in pairs]
        tinv = [tinv[p] + prod[p][:L] for p in pairs]
        npow = [prod[p][L:] for p in pairs]
        span *= 2
    tinv = [tinv[p] + _dot(tinv[p], cat(npow[p])) for p in pairs]
    wu = [_dot(tinv[p], jnp.concatenate([cat(a_hat[:, sls[p]]), cat(av[p][:L])], axis=1))
          for p in pairs]
    qy = [_dot(a_rb[p], jnp.concatenate([cat(wu[p][:, :LANES]), cat(wu[p][:, LANES:])], axis=1))
          for p in pairs]
    mc = [_dot_tn(stack(b_til[:, sls[p]], k_til[:, sls[p]]),
                  jnp.concatenate([wu[p].astype(BF16),
                                   jnp.concatenate([jnp.zeros((L, LANES), BF16),
                                                    v[:, sls[p]].astype(BF16)], axis=1)], axis=0))
          for p in pairs]
    ys = []
    for p in pairs:
        dl = jnp.broadcast_to(d_last[:, sls[p]], (LANES, LANES))
        m_mat = jnp.where(same_head, mc[p][:, :LANES], 0.0) + jnp.where(pi == qi, dl, 0.0)
        c_mat = jnp.where(same_head, mc[p][:, LANES:], 0.0)
        q_t = r_hat[:, sls[p]] + qy[p][:, :LANES]
        out = _dot(stack(q_t, m_mat), s_scr[p])
        ys.append(out[:L] + qy[p][:, LANES:] + av[p][L:])
        s_scr[p] = out[L:] + c_mat

    y = jnp.concatenate(ys, axis=1)
    vec = vec_ref[...]
    ones = _head_ones()
    inv_n = 1.0 / HEAD_DIM
    mu = _head_sum(y, ones) * inv_n
    dy = y - mu
    var = _head_sum(dy * dy, ones) * inv_n
    gn = dy * lax.rsqrt(var + RWKV_GN_EPS) * vec[1:2] + vec[2:3]
    bonus = _head_sum(r * k * vec[0:1], ones) * v
    y_ref[...] = ((gn + bonus) * g_ref[...]).astype(BF16)


def _wkv(r, k, v, a, b, lw, g, vec):
    bsz, s, d = r.shape
    L = WKV_CHUNK
    spec = pl.BlockSpec((None, L, d), lambda bb, c: (bb, c, 0))
    return pl.pallas_call(
        _wkv_kernel,
        grid=(bsz, s // L),
        in_specs=[spec] * 7 + [pl.BlockSpec((SUBLANES, d), lambda bb, c: (0, 0))],
        out_specs=spec,
        out_shape=jax.ShapeDtypeStruct((bsz, s, d), BF16),
        scratch_shapes=[pltpu.VMEM((d // LANES, LANES, LANES), F32)],
        compiler_params=_cparams(("arbitrary", "arbitrary")),
        name="wkv7",
    )(r, k, v, a, b, lw, g, vec)


def _interleave(ref, scr):
    dil, n, w = ref.shape
    if dil == 1:
        return ref[0]
    for r in range(dil):
        for c in range(w // LANES):
            scr[c, pl.ds(r, n, stride=dil), :] = ref[r, :, c * LANES:(c + 1) * LANES]
    return jnp.concatenate([scr[c] for c in range(w // LANES)], axis=1)


def _merge_kernel(x_ref, ya_ref, yc_ref, o0_ref, o1_ref, o2_ref, l0_ref, l1_ref, l2_ref,
                  ga_ref, gb_ref, gc_ref, gt_ref, pa_ref, pb_ref, pc_ref, wo_ref, ln_ref, out_ref,
                  so1, so2, sl1, sl2):
    l0 = l0_ref[0]
    l1 = _interleave(l1_ref, sl1)
    l2 = _interleave(l2_ref, sl2)
    o0 = o0_ref[0]
    o1 = _interleave(o1_ref, so1)
    o2 = _interleave(o2_ref, so2)
    lm = jnp.maximum(jnp.maximum(l0, l1), l2)
    e0 = jnp.exp(l0 - lm)
    e1 = jnp.exp(l1 - lm)
    e2 = jnp.exp(l2 - lm)
    yb = (e0 * o0 + e1 * o1 + e2 * o2) / (e0 + e1 + e2)
    merged = (jax.nn.sigmoid(ga_ref[...]) * jnp.dot(ya_ref[...], pa_ref[...], preferred_element_type=F32)
              + jax.nn.sigmoid(gb_ref[...]) * _dot(yb, pb_ref[...])
              + jax.nn.sigmoid(gc_ref[...]) * jnp.dot(yc_ref[...], pc_ref[...], preferred_element_type=F32))
    y = _dot(merged, wo_ref[...])
    ln = ln_ref[...]
    out_ref[...] = _layer_norm(ALPHA * x_ref[...] + (1.0 + gt_ref[...]) * y, ln[0:1], ln[1:2])


def _merge(x, z, ya, yc, att, gt, pa, pb, pc, wo, ln):
    bsz, s, d = x.shape
    tm = 256
    wa = D_ATTN_OUT
    row = lambda w, off=0: pl.BlockSpec((None, tm, w), lambda b, i: (b, i, off // w))
    const = lambda shp: pl.BlockSpec(shp, lambda b, i: (0,) * len(shp))
    res = lambda dil: pl.BlockSpec((None, dil, tm // dil, wa), lambda b, i: (b, 0, i, 0))
    (o0, l0), (o1, l1), (o2, l2) = att
    att_specs = [res(dil) for _, dil in ATTN_GROUPS] * 2
    return pl.pallas_call(
        _merge_kernel,
        grid=(bsz, s // tm),
        scratch_shapes=[pltpu.VMEM((wa // LANES, tm, LANES), F32)] * 4,
        in_specs=[row(d), row(d), row(d)] + att_specs
                 + [row(d, OFF_GATE), row(d, OFF_GATE + d), row(d, OFF_GATE + 2 * d),
                    pl.BlockSpec((None, 1, d), lambda b, i: (b, 0, 0)),
                    const((d, d)), const((wa, d)), const((d, d)), const((d, d)),
                    const((SUBLANES, d))],
        out_specs=row(d),
        out_shape=jax.ShapeDtypeStruct((bsz, s, d), F32),
        compiler_params=_cparams(("arbitrary", "arbitrary")),
        name="merge_proj_ln",
    )(x, ya, yc, o0, o1, o2, l0, l1, l2, z, z, z, gt, pa, pb, pc, wo, ln)


def _ffn_kernel(x_ref, halo_ref, sc_ref, sh_ref, gt_ref, wg_ref, wv_ref, cwg_ref, cwv_ref,
                wd_ref, ln_ref, out_ref, h_scr, ug_scr, uv_scr, acc):
    i = pl.program_id(1)
    f = pl.program_id(2)
    tm = x_ref.shape[0]
    hr = halo_ref.shape[0]

    @pl.when(f == 0)
    def _():
        sc = 1.0 + sc_ref[...]
        sh = sh_ref[...]
        hh = jnp.where(i == 0, 0.0, halo_ref[...] * sc + sh)
        h_scr[0:hr, :] = hh.astype(BF16)
        h_scr[hr:hr + tm, :] = (x_ref[...] * sc + sh).astype(BF16)

    h = h_scr[...]
    ug_scr[...] = jnp.dot(h, wg_ref[...], preferred_element_type=F32)
    uv_scr[...] = jnp.dot(h, wv_ref[...], preferred_element_type=F32)

    def conv(u_scr, cw):
        out = cw[CONV_F:CONV_F + 1]
        for j in range(CONV_F):
            out = out + cw[j:j + 1] * u_scr[hr - j:hr - j + tm, :]
        return out

    cg = conv(ug_scr, cwg_ref[...])
    cv = conv(uv_scr, cwv_ref[...])
    act = cg * jax.nn.sigmoid(cg) * cv
    part = _dot(act, wd_ref[...])

    @pl.when(f == 0)
    def _():
        acc[...] = part

    @pl.when(f > 0)
    def _():
        acc[...] += part

    @pl.when(f == pl.num_programs(2) - 1)
    def _():
        ln = ln_ref[...]
        out_ref[...] = _layer_norm(ALPHA * x_ref[...] + (1.0 + gt_ref[...]) * acc[...],
                                   ln[0:1], ln[1:2])


def _conv_ffn(x, sc, sh, gt, w_up, conv_wb, w_down, ln):
    bsz, s, d = x.shape
    tm, fc = 512, 1408
    nf = D_FF // fc
    hr = BF16_ROWS
    mod = pl.BlockSpec((None, 1, d), lambda b, i, f: (b, 0, 0))
    return pl.pallas_call(
        _ffn_kernel,
        grid=(bsz, s // tm, nf),
        in_specs=[pl.BlockSpec((None, tm, d), lambda b, i, f: (b, i, 0)),
                  pl.BlockSpec((None, hr, d), lambda b, i, f: (b, jnp.maximum(i * (tm // hr) - 1, 0), 0)),
                  mod, mod, mod,
                  pl.BlockSpec((d, fc), lambda b, i, f: (0, f)),
                  pl.BlockSpec((d, fc), lambda b, i, f: (0, nf + f)),
                  pl.BlockSpec((SUBLANES, fc), lambda b, i, f: (0, f)),
                  pl.BlockSpec((SUBLANES, fc), lambda b, i, f: (0, nf + f)),
                  pl.BlockSpec((fc, d), lambda b, i, f: (f, 0)),
                  pl.BlockSpec((SUBLANES, d), lambda b, i, f: (0, 0))],
        out_specs=pl.BlockSpec((None, tm, d), lambda b, i, f: (b, i, 0)),
        out_shape=jax.ShapeDtypeStruct((bsz, s, d), F32),
        scratch_shapes=[pltpu.VMEM((tm + hr, d), BF16),
                        pltpu.VMEM((tm + hr, fc), F32),
                        pltpu.VMEM((tm + hr, fc), F32),
                        pltpu.VMEM((tm, d), F32)],
        compiler_params=_cparams(("arbitrary", "arbitrary", "arbitrary")),
        name="conv_ffn_ln",
    )(x, x, sc, sh, gt, w_up, w_up, conv_wb, conv_wb, w_down, ln)


def _rows(vectors, n_rows=SUBLANES):
    m = jnp.stack([v.astype(F32) for v in vectors], axis=0)
    return jnp.pad(m, ((0, n_rows - m.shape[0]), (0, 0)))


def _block_diag(w, per):
    g, n, _ = w.shape
    w = w.reshape(g // per, per, n, n)
    eye = jnp.eye(per, dtype=w.dtype)
    out = jnp.einsum('cpij,pq->cpiqj', w, eye)
    return out.reshape(g // per, per * n, per * n)


def kernel(x, c, positions, mod_w, mod_b, w_in, w_in_vres, conv_a_w, conv_a_b, lru_wa, lru_ba, lru_wx, lru_bx, lru_lambda, rwkv_mu, mu_vres, w0, w2, a0, a2, g2, v0, v2, k_k, k_a, r_k, ln_x_w, ln_x_b, proj_a, proj_b, proj_c, w_o, ln1_w, ln1_b, ffn_up, ffn_conv_w, ffn_conv_b, ffn_down, ln2_w, ln2_b):
    bsz, s, d = x.shape
    mod = _modulation(c, mod_w, mod_b)
    half = ROT_DIM // 2
    inv_freq = ROPE_THETA ** (-jnp.arange(half, dtype=F32) / half)
    lane = jnp.arange(LANES)
    freq_lane = jnp.where((lane % HEAD_DIM) < ROT_DIM, inv_freq[lane % half], 0.0).reshape(1, LANES)
    rope_tabs = _rope_tables(positions.reshape(bsz, s, 1), freq_lane)
    zero_d = jnp.zeros((d,), F32)

    v_first = None
    for l in range(DEPTH):
        m6 = mod[l].reshape(bsz, 6, 1, d)
        sh1, sc1, gt1, sh2, sc2, gt2 = (m6[:, i] for i in range(6))

        wl = w_in[l]
        w_perm = jnp.concatenate([wl[:, 0:2048], wl[:, 4352:7424], wl[:, 7680:10752],
                                  wl[:, 7424:7680], wl[:, 2048:4352]], axis=1).astype(BF16)
        if l == 0:
            w_vres = None
        else:
            w_vres = jnp.pad(w_in_vres[l - 1], ((0, 0), (0, LANES - MV_LORA))).astype(BF16)
        z, qkv_groups, zv1 = _in_projection(x, sc1, sh1, w_perm, w_vres, rope_tabs)

        per = 256 // RNN_BLOCK
        y_a = _rglru(z, conv_a_w[l],
                     _rows([conv_a_b[l], lru_ba[l], lru_bx[l], lru_lambda[l]]),
                     _block_diag(lru_wa[l], per).astype(BF16),
                     _block_diag(lru_wx[l], per).astype(BF16))

        att = [_attention_group(qkv_groups[g], g) for g in range(N_GROUPS)]

        mu = rwkv_mu[l]
        vec = _rows([mu[0:1024], mu[1024:2048], mu[2048:3072], w0[l], a0[l], k_k[l], k_a[l],
                     v0[l - 1] if l > 0 else zero_d])
        mu_l = mu[3072:3328].reshape(1, 2 * LANES)
        w2p = jnp.pad(w2[l], ((0, LANES - LORA_W), (0, 0))).astype(BF16)
        a2p = jnp.pad(a2[l], ((LORA_W, 0), (0, 0))).astype(BF16)
        if l == 0:
            prep = _rwkv_prep(z, None, None, vec, mu_l, w2p, a2p, g2[l].astype(BF16), None, None)
        else:
            mu_v1 = jnp.pad(mu_vres[l - 1], (0, LANES - MV_LORA)).reshape(1, LANES)
            v2p = jnp.pad(v2[l - 1], ((0, LANES - MV_LORA), (0, 0))).astype(BF16)
            prep = _rwkv_prep(z, zv1, v_first, vec, mu_l, w2p, a2p, g2[l].astype(BF16), mu_v1, v2p)
        r_, k_, v_, a_, b_, lw_, g_ = prep
        if l == 0:
            v_first = v_
        y_c = _wkv(r_, k_, v_, a_, b_, lw_, g_,
                   _rows([r_k[l].reshape(-1), ln_x_w[l], ln_x_b[l]]))

        x = _merge(x, z, y_a, y_c, att, gt1, proj_a[l].astype(BF16), proj_b[l].astype(BF16),
                   proj_c[l].astype(BF16), w_o[l].astype(BF16), _rows([ln1_w[l], ln1_b[l]]))

        conv_wb = jnp.concatenate([ffn_conv_w[l], ffn_conv_b[l][None, :]], axis=0)
        conv_wb = jnp.pad(conv_wb, ((0, SUBLANES - conv_wb.shape[0]), (0, 0)))
        x = _conv_ffn(x, sc2, sh2, gt2, ffn_up[l].astype(BF16), conv_wb,
                      ffn_down[l].astype(BF16), _rows([ln2_w[l], ln2_b[l]]))
    return x
```

```python
import functools

import jax
import jax.numpy as jnp
from jax import lax
from jax.experimental import pallas as pl
from jax.experimental.pallas import tpu as pltpu

F32 = jnp.float32
BF16 = jnp.bfloat16

D_MODEL = 1024
DEPTH = 2
D_RNN = 1024
RNN_BLOCK = 64
CONV_A = 4
LRU_C = 8.0
HEAD_DIM = 64
ATTN_GROUPS = ((128, 1), (512, 4), (2048, 16))
HEADS_PER_GROUP = 4
N_GROUPS = 3
D_ATTN = 768
D_ATTN_OUT = 256
ROT_DIM = 16
ROPE_THETA = 500000.0
D_RWKV = 1024
LORA_W = 64
LORA_A = 64
LORA_G = 128
MV_LORA = 32
RWKV_GN_EPS = 64e-5
DECAY_SCALE = 0.6065306597126334
D_FF = 2816
CONV_F = 3
ALPHA = (2 * DEPTH) ** 0.25
LN_EPS = 1e-5

LANES = 128
SUBLANES = 8
BF16_ROWS = 16

OFF_XA = 0
OFF_GA = 1024
OFF_R = 2048
OFF_K = 3072
OFF_V = 4096
OFF_GATE = 5120
OFF_LORA = 8192
N_Z = 8448
MAIN_TILE = 2816

FFN_SLABS = ((0, 1536), (1536, 1280))
WKV_CHUNK = 64
ATTN_BLOCK = 128
VMEM_LIMIT = 52 * 1024 * 1024


def _cparams(sem):
    return pltpu.CompilerParams(dimension_semantics=sem, vmem_limit_bytes=VMEM_LIMIT)


def _dot(a, b):
    return jnp.dot(a.astype(BF16), b.astype(BF16), preferred_element_type=F32)


def _dot_nt(a, b):
    return lax.dot_general(a.astype(BF16), b.astype(BF16), (((1,), (1,)), ((), ())),
                           preferred_element_type=F32)


def _dot_tn(a, b):
    return lax.dot_general(a.astype(BF16), b.astype(BF16), (((0,), (0,)), ((), ())),
                           preferred_element_type=F32)


def _split3(x):
    p0 = x.astype(BF16)
    r1 = x - p0.astype(F32)
    p1 = r1.astype(BF16)
    p2 = (r1 - p1.astype(F32)).astype(BF16)
    return p0, p1, p2


def _head_ones():
    i = lax.broadcasted_iota(jnp.int32, (LANES, LANES), 0) // HEAD_DIM
    j = lax.broadcasted_iota(jnp.int32, (LANES, LANES), 1) // HEAD_DIM
    return jnp.where(i == j, 1.0, 0.0).astype(BF16)


def _head_sum(x, ones):
    rows = x.shape[0]
    n = x.shape[1] // LANES
    xs = jnp.concatenate([x[:, c * LANES:(c + 1) * LANES] for c in range(n)], axis=0)
    hi = xs.astype(BF16)
    lo = (xs - hi.astype(F32)).astype(BF16)
    out = jnp.dot(jnp.concatenate([hi, lo], axis=0), ones, preferred_element_type=F32)
    tot = out[:n * rows] + out[n * rows:]
    return jnp.concatenate([tot[c * rows:(c + 1) * rows] for c in range(n)], axis=1)


def _softplus(x):
    return jnp.maximum(x, 0.0) + jnp.log1p(jnp.exp(-jnp.abs(x)))


def _sigmoid(x):
    return 0.5 * jnp.tanh(0.5 * x) + 0.5


def _layer_norm(x, w, b):
    mu = jnp.mean(x, axis=-1, keepdims=True)
    d = x - mu
    var = jnp.mean(d * d, axis=-1, keepdims=True)
    return d * lax.rsqrt(var + LN_EPS) * w + b


def _mod_kernel(c_ref, w_ref, b_ref, o_ref):
    c = c_ref[...]
    s = c * jax.nn.sigmoid(c)
    s0, s1, s2 = _split3(s)
    w0, w1, w2 = _split3(w_ref[...])
    acc = jnp.dot(s0, w0, preferred_element_type=F32)
    acc += jnp.dot(s0, w1, preferred_element_type=F32) + jnp.dot(s1, w0, preferred_element_type=F32)
    acc += (jnp.dot(s1, w1, preferred_element_type=F32) + jnp.dot(s0, w2, preferred_element_type=F32)
            + jnp.dot(s2, w0, preferred_element_type=F32))
    o_ref[...] = acc + b_ref[...]


def _modulation(c, mod_w, mod_b):
    depth, d, n = mod_w.shape
    bsz = c.shape[0]
    rows = BF16_ROWS
    tn = 1536
    c_pad = jnp.pad(c, ((0, rows - bsz), (0, 0)))
    out = pl.pallas_call(
        _mod_kernel,
        grid=(depth, n // tn),
        in_specs=[pl.BlockSpec((rows, d), lambda l, j: (0, 0)),
                  pl.BlockSpec((None, d, tn), lambda l, j: (l, 0, j)),
                  pl.BlockSpec((None, 1, tn), lambda l, j: (l, 0, j))],
        out_specs=pl.BlockSpec((None, rows, tn), lambda l, j: (l, 0, j)),
        out_shape=jax.ShapeDtypeStruct((depth, rows, n), F32),
        compiler_params=_cparams(("arbitrary", "arbitrary")),
        name="adaln_mod",
    )(c_pad, mod_w, mod_b.reshape(depth, 1, n))
    return out[:, :bsz]


def _main_proj_kernel(x_ref, sc_ref, sh_ref, w_ref, z_ref):
    h = (x_ref[...] * (1.0 + sc_ref[...]) + sh_ref[...]).astype(BF16)
    z_ref[...] = jnp.dot(h, w_ref[...], preferred_element_type=F32)


def _main_projection(x, sc, sh, w):
    bsz, s, d = x.shape
    n = w.shape[1]
    tm, tn = 512, MAIN_TILE
    return pl.pallas_call(
        _main_proj_kernel,
        grid=(n // tn, bsz, s // tm),
        in_specs=[pl.BlockSpec((None, tm, d), lambda j, b, i: (b, i, 0)),
                  pl.BlockSpec((None, 1, d), lambda j, b, i: (b, 0, 0)),
                  pl.BlockSpec((None, 1, d), lambda j, b, i: (b, 0, 0)),
                  pl.BlockSpec((d, tn), lambda j, b, i: (0, j))],
        out_specs=pl.BlockSpec((None, tm, tn), lambda j, b, i: (b, i, j)),
        out_shape=jax.ShapeDtypeStruct((bsz, s, n), F32),
        compiler_params=_cparams(("arbitrary", "arbitrary", "arbitrary")),
        name="main_projection",
    )(x, sc, sh, w)


def _attn_proj_kernel(x_ref, sc_ref, sh_ref, w_ref, cos_ref, sa_ref, sb_ref, *rest, has_vres):
    if has_vres:
        wv_ref, g0_ref, g1_ref, g2_ref, zv_ref, r_scr = rest
    else:
        g0_ref, g1_ref, g2_ref, r_scr = rest
    tm = x_ref.shape[0]
    width = HEADS_PER_GROUP * HEAD_DIM
    half = ROT_DIM // 2
    h = (x_ref[...] * (1.0 + sc_ref[...]) + sh_ref[...]).astype(BF16)
    if has_vres:
        zv_ref[...] = jnp.dot(h, wv_ref[...], preferred_element_type=F32)
    acc = jnp.dot(h, w_ref[...], preferred_element_type=F32)
    n_slab = D_ATTN // LANES
    for sec in range(3):
        scale = HEAD_DIM ** -0.5 if sec == 0 else 1.0
        for c in range(n_slab):
            lo = sec * D_ATTN + c * LANES
            t = acc[:, lo:lo + LANES]
            if sec < 2:
                t = (t * cos_ref[...] + pltpu.roll(t, half, 1) * sa_ref[...]
                     + pltpu.roll(t, LANES - half, 1) * sb_ref[...]) * scale
            r_scr[sec * n_slab + c] = t
    per = width // LANES
    for sec in range(3):
        for g, g_ref in enumerate((g0_ref, g1_ref, g2_ref)):
            dil = g_ref.shape[0]
            for r in range(dil):
                rows = pl.ds(r, tm // dil, stride=dil) if dil > 1 else slice(None)
                for c in range(per):
                    lo = sec * width + c * LANES
                    g_ref[r, :, lo:lo + LANES] = r_scr[sec * n_slab + g * per + c, rows, :].astype(BF16)


def _attn_projection(x, sc, sh, w, w_vres, rope_tabs):
    bsz, s, d = x.shape
    n = w.shape[1]
    tm = 512
    has_vres = w_vres is not None
    tab = pl.BlockSpec((None, tm, LANES), lambda b, i: (b, i, 0))
    in_specs = [pl.BlockSpec((None, tm, d), lambda b, i: (b, i, 0)),
                pl.BlockSpec((None, 1, d), lambda b, i: (b, 0, 0)),
                pl.BlockSpec((None, 1, d), lambda b, i: (b, 0, 0)),
                pl.BlockSpec((d, n), lambda b, i: (0, 0)),
                tab, tab, tab]
    out_specs, out_shape = [], []
    for _, dil in ATTN_GROUPS:
        out_specs.append(pl.BlockSpec((None, dil, tm // dil, D_ATTN), lambda b, i: (b, 0, i, 0)))
        out_shape.append(jax.ShapeDtypeStruct((bsz, dil, s // dil, D_ATTN), BF16))
    args = [x, sc, sh, w, *rope_tabs]
    if has_vres:
        in_specs.append(pl.BlockSpec((d, LANES), lambda b, i: (0, 0)))
        out_specs.append(pl.BlockSpec((None, tm, LANES), lambda b, i: (b, i, 0)))
        out_shape.append(jax.ShapeDtypeStruct((bsz, s, LANES), F32))
        args.append(w_vres)
    outs = pl.pallas_call(
        functools.partial(_attn_proj_kernel, has_vres=has_vres),
        grid=(bsz, s // tm),
        in_specs=in_specs,
        out_specs=out_specs,
        out_shape=out_shape,
        scratch_shapes=[pltpu.VMEM((n // LANES, tm, LANES), F32)],
        compiler_params=_cparams(("arbitrary", "arbitrary")),
        name="attn_projection",
    )(*args)
    return outs[0:3], (outs[3] if has_vres else None)


def _rglru_kernel(xa_ref, ga_ref, cw_ref, vec_ref, wa_ref, wx_ref, y_ref, xbuf, hc):
    t = pl.program_id(2)
    tt, cwid = xa_ref.shape

    @pl.when(t == 0)
    def _():
        xbuf[0:SUBLANES, :] = jnp.zeros((SUBLANES, cwid), F32)
        hc[...] = jnp.zeros_like(hc)

    xa = xa_ref[...]
    xbuf[SUBLANES:SUBLANES + tt, :] = xa
    cw = cw_ref[...]
    vec = vec_ref[...]
    xc = vec[0:1] + cw[0:1] * xa
    for j in range(1, CONV_A):
        xc = xc + cw[j:j + 1] * xbuf[SUBLANES - j:SUBLANES - j + tt, :]
    xbuf[0:SUBLANES, :] = xbuf[tt:tt + SUBLANES, :]

    xb = xc.astype(BF16)
    r = jax.nn.sigmoid(jnp.dot(xb, wa_ref[...], preferred_element_type=F32) + vec[1:2])
    i = jax.nn.sigmoid(jnp.dot(xb, wx_ref[...], preferred_element_type=F32) + vec[2:3])
    log_a = (-LRU_C) * r * _softplus(-vec[3:4])
    a = jnp.exp(log_a)
    b = jnp.sqrt(-jnp.tanh(log_a) * (a * a + 1.0)) * (i * xc)

    ng = tt // SUBLANES
    a = a.reshape(ng, SUBLANES, cwid)
    b = b.reshape(ng, SUBLANES, cwid)
    rows = lax.broadcasted_iota(jnp.int32, (ng, SUBLANES, cwid), 1)
    step = 1
    while step < SUBLANES:
        a_sh = pltpu.roll(a, step, 1)
        b_sh = pltpu.roll(b, step, 1)
        m = rows >= step
        b = jnp.where(m, a * b_sh + b, b)
        a = jnp.where(m, a * a_sh, a)
        step *= 2
    g = ga_ref[...]
    gelu = 0.5 * g * (1.0 + jnp.tanh(0.7978845608028654 * (g + 0.044715 * (g * g * g))))
    carry = hc[0:1, :]
    hs = []
    for i in range(ng):
        h = a[i] * carry + b[i]
        carry = h[SUBLANES - 1:SUBLANES, :]
        hs.append(h)
    hc[0:1, :] = carry
    y_ref[...] = (jnp.concatenate(hs, axis=0) * gelu).astype(BF16)


def _rglru(z, conv_w, vec, wa_bd, wx_bd):
    bsz, s, _ = z.shape
    tt, cwid = 256, 256
    nc = D_RNN // cwid
    return pl.pallas_call(
        _rglru_kernel,
        grid=(bsz, nc, s // tt),
        in_specs=[pl.BlockSpec((None, tt, cwid), lambda b, c, t: (b, t, OFF_XA // cwid + c)),
                  pl.BlockSpec((None, tt, cwid), lambda b, c, t: (b, t, OFF_GA // cwid + c)),
                  pl.BlockSpec((CONV_A, cwid), lambda b, c, t: (0, c)),
                  pl.BlockSpec((SUBLANES, cwid), lambda b, c, t: (0, c)),
                  pl.BlockSpec((None, cwid, cwid), lambda b, c, t: (c, 0, 0)),
                  pl.BlockSpec((None, cwid, cwid), lambda b, c, t: (c, 0, 0))],
        out_specs=pl.BlockSpec((None, tt, cwid), lambda b, c, t: (b, t, c)),
        out_shape=jax.ShapeDtypeStruct((bsz, s, D_RNN), BF16),
        scratch_shapes=[pltpu.VMEM((tt + SUBLANES, cwid), F32), pltpu.VMEM((SUBLANES, cwid), F32)],
        compiler_params=_cparams(("arbitrary", "arbitrary", "arbitrary")),
        name="rglru",
    )(z, z, conv_w, vec, wa_bd, wx_bd)


def _rope_table_kernel(pos_ref, freq_ref, cos_ref, sa_ref, sb_ref):
    tt = pos_ref.shape[0]
    ang = pos_ref[...].astype(F32) * freq_ref[...]
    lane = lax.broadcasted_iota(jnp.int32, (tt, LANES), 1) % HEAD_DIM
    sn = jnp.sin(ang)
    half = ROT_DIM // 2
    cos_ref[...] = jnp.cos(ang)
    sa_ref[...] = jnp.where((lane >= half) & (lane < ROT_DIM), sn, 0.0)
    sb_ref[...] = jnp.where(lane < half, -sn, 0.0)


def _rope_tables(pos3, freq_lane):
    bsz, s, _ = pos3.shape
    tt = 1024
    out = pl.BlockSpec((None, tt, LANES), lambda b, t: (b, t, 0))
    return pl.pallas_call(
        _rope_table_kernel,
        grid=(bsz, s // tt),
        in_specs=[pl.BlockSpec((None, tt, 1), lambda b, t: (b, t, 0)),
                  pl.BlockSpec((1, LANES), lambda b, t: (0, 0))],
        out_specs=[out] * 3,
        out_shape=[jax.ShapeDtypeStruct((bsz, s, LANES), F32)] * 3,
        compiler_params=_cparams(("arbitrary", "arbitrary")),
        name="rope_tables",
    )(pos3, freq_lane)


def _attn_kernel(q_ref, kp_ref, kc_ref, vp_ref, vc_ref, o_ref, lse_ref):
    mb = pl.program_id(2)
    nq, width = q_ref.shape
    nh = HEADS_PER_GROUP
    q = q_ref[...]
    k = jnp.concatenate([kp_ref[...], kc_ref[...]], axis=0)
    v = jnp.concatenate([vp_ref[...], vc_ref[...]], axis=0)
    nk = 2 * nq
    head = lax.broadcasted_iota(jnp.int32, (nq, width), 1) // HEAD_DIM
    qcat = jnp.concatenate([jnp.where(head == h, q, jnp.zeros_like(q)) for h in range(nh)], axis=0)
    s = lax.dot_general(qcat, k, (((1,), (1,)), ((), ())), preferred_element_type=F32)
    qi = lax.broadcasted_iota(jnp.int32, (nh * nq, nk), 0) % nq
    kj = lax.broadcasted_iota(jnp.int32, (nh * nq, nk), 1)
    valid = (kj >= qi) & (kj <= qi + nq) & ((mb > 0) | (kj >= nq))
    s = jnp.where(valid, s, -1e30)
    m = jnp.max(s, axis=1, keepdims=True)
    p = jnp.exp(s - m)
    l = jnp.sum(p, axis=1, keepdims=True)
    pv = jnp.dot(p.astype(BF16), v, preferred_element_type=F32) / l
    lse_rows = m + jnp.log(l)
    o = jnp.zeros((nq, width), F32)
    lse = jnp.zeros((nq, width), F32)
    for h in range(nh):
        hm = head == h
        rows = slice(h * nq, (h + 1) * nq)
        o = jnp.where(hm, pv[rows], o)
        lse = jnp.where(hm, lse_rows[rows], lse)
    o_ref[...] = o
    lse_ref[...] = lse


def _attention_group(qkv, g):
    bsz, dil, sub, _ = qkv.shape
    width = HEADS_PER_GROUP * HEAD_DIM
    nb = ATTN_BLOCK

    def cur(sec):
        return lambda b, r, m: (b, r, m, sec)

    def prev(sec):
        return lambda b, r, m: (b, r, jnp.maximum(m - 1, 0), sec)

    blk = (None, None, nb, width)
    return pl.pallas_call(
        _attn_kernel,
        grid=(bsz, dil, sub // nb),
        in_specs=[pl.BlockSpec(blk, cur(0)),
                  pl.BlockSpec(blk, prev(1)), pl.BlockSpec(blk, cur(1)),
                  pl.BlockSpec(blk, prev(2)), pl.BlockSpec(blk, cur(2))],
        out_specs=[pl.BlockSpec(blk, cur(0)), pl.BlockSpec(blk, cur(0))],
        out_shape=[jax.ShapeDtypeStruct((bsz, dil, sub, width), F32)] * 2,
        compiler_params=_cparams(("arbitrary", "arbitrary", "arbitrary")),
        name=f"dilated_attention_g{g}",
    )(qkv, qkv, qkv, qkv, qkv)


def _token_shift(z, halo, first, mu):
    ext = jnp.concatenate([jnp.where(first, 0.0, halo), z], axis=0)
    zp = pltpu.roll(ext, 1, 0)[SUBLANES:]
    return z + (zp - z) * mu


def _rwkv_prep_kernel(*refs, has_vres):
    if has_vres:
        (zr_ref, zk_ref, zv_ref, zl_ref, hr_ref, hk_ref, hv_ref, hl_ref, vec_ref, mul_ref,
         w2_ref, a2_ref, g2_ref, zv1_ref, hv1_ref, vf_ref, muv_ref, v2_ref,
         r_o, k_o, v_o, a_o, b_o, lw_o, g_o) = refs
    else:
        (zr_ref, zk_ref, zv_ref, zl_ref, hr_ref, hk_ref, hv_ref, hl_ref, vec_ref, mul_ref,
         w2_ref, a2_ref, g2_ref,
         r_o, k_o, v_o, a_o, b_o, lw_o, g_o) = refs
    first = pl.program_id(1) == 0
    vec = vec_ref[...]
    r = _token_shift(zr_ref[...], hr_ref[...], first, vec[0:1])
    k = _token_shift(zk_ref[...], hk_ref[...], first, vec[1:2])
    v = _token_shift(zv_ref[...], hv_ref[...], first, vec[2:3])
    lora = _token_shift(zl_ref[...], hl_ref[...], first, mul_ref[...])
    wa_in = lora[:, 0:LANES]
    w_pre = vec[3:4] + _dot(jnp.tanh(wa_in), w2_ref[...])
    lw_o[...] = (-DECAY_SCALE) * _sigmoid(w_pre)
    alpha = _sigmoid(vec[4:5] + _dot(wa_in, a2_ref[...]))
    g_o[...] = _dot(_sigmoid(lora[:, LANES:2 * LANES]), g2_ref[...])
    if has_vres:
        zv1 = _token_shift(zv1_ref[...], hv1_ref[...], first, muv_ref[...])
        mix = _sigmoid(vec[7:8] + _dot(zv1, v2_ref[...]))
        v = v + (vf_ref[...] - v) * mix
    ones = _head_ones()
    kk = k * vec[5:6]
    kk = kk * lax.rsqrt(_head_sum(kk * kk, ones) + 1e-12)
    r_o[...] = r
    k_o[...] = k * (1.0 + (alpha - 1.0) * vec[6:7])
    v_o[...] = v
    a_o[...] = -kk
    b_o[...] = kk * alpha


def _rwkv_prep(z, zv1, v_first, vec, mu_l, w2p, a2p, g2, mu_v1, v2p):
    bsz, s, _ = z.shape
    tt = 256
    d = D_RWKV
    has_vres = zv1 is not None
    hstep = tt // SUBLANES

    def col(off, w):
        return lambda b, t: (b, t, off // w)

    def hcol(off, w):
        return lambda b, t: (b, jnp.maximum(t * hstep - 1, 0), off // w)

    lw = 2 * LANES
    in_specs = [pl.BlockSpec((None, tt, d), col(OFF_R, d)),
                pl.BlockSpec((None, tt, d), col(OFF_K, d)),
                pl.BlockSpec((None, tt, d), col(OFF_V, d)),
                pl.BlockSpec((None, tt, lw), col(OFF_LORA, lw)),
                pl.BlockSpec((None, SUBLANES, d), hcol(OFF_R, d)),
                pl.BlockSpec((None, SUBLANES, d), hcol(OFF_K, d)),
                pl.BlockSpec((None, SUBLANES, d), hcol(OFF_V, d)),
                pl.BlockSpec((None, SUBLANES, lw), hcol(OFF_LORA, lw)),
                pl.BlockSpec((SUBLANES, d), lambda b, t: (0, 0)),
                pl.BlockSpec((1, lw), lambda b, t: (0, 0)),
                pl.BlockSpec((LANES, d), lambda b, t: (0, 0)),
                pl.BlockSpec((LANES, d), lambda b, t: (0, 0)),
                pl.BlockSpec((LANES, d), lambda b, t: (0, 0))]
    args = [z, z, z, z, z, z, z, z, vec, mu_l, w2p, a2p, g2]
    if has_vres:
        in_specs += [pl.BlockSpec((None, tt, LANES), lambda b, t: (b, t, 0)),
                     pl.BlockSpec((None, SUBLANES, LANES),
                                  lambda b, t: (b, jnp.maximum(t * hstep - 1, 0), 0)),
                     pl.BlockSpec((None, tt, d), lambda b, t: (b, t, 0)),
                     pl.BlockSpec((1, LANES), lambda b, t: (0, 0)),
                     pl.BlockSpec((LANES, d), lambda b, t: (0, 0))]
        args += [zv1, zv1, v_first, mu_v1, v2p]
    out_spec = pl.BlockSpec((None, tt, d), lambda b, t: (b, t, 0))
    return pl.pallas_call(
        functools.partial(_rwkv_prep_kernel, has_vres=has_vres),
        grid=(bsz, s // tt),
        in_specs=in_specs,
        out_specs=[out_spec] * 7,
        out_shape=[jax.ShapeDtypeStruct((bsz, s, d), F32)] * 7,
        compiler_params=_cparams(("arbitrary", "arbitrary")),
        name="rwkv_prep",
    )(*args)


def _wkv_kernel(r_ref, k_ref, v_ref, a_ref, b_ref, lw_ref, g_ref, vec_ref, y_ref, s_scr):
    L = r_ref.shape[0]
    n_pairs = r_ref.shape[1] // LANES

    @pl.when(pl.program_id(1) == 0)
    def _():
        s_scr[...] = jnp.zeros_like(s_scr)

    lw = lw_ref[...]
    ti = lax.broadcasted_iota(jnp.int32, (L, L), 0)
    tj = lax.broadcasted_iota(jnp.int32, (L, L), 1)
    tri = jnp.where(ti >= tj, 1.0, 0.0).astype(BF16)
    l0, l1, l2 = _split3(lw)
    lc = (jnp.dot(tri, l0, preferred_element_type=F32) + jnp.dot(tri, l1, preferred_element_type=F32)
          + jnp.dot(tri, l2, preferred_element_type=F32))
    lc_last = lc[L - 1:L, :]
    r = r_ref[...]
    k = k_ref[...]
    v = v_ref[...]
    a = a_ref[...]
    b = b_ref[...]
    e_inv = jnp.exp(-lc)
    e_tail = jnp.exp(lc_last - lc)
    a_hat = a * jnp.exp(lc - lw)
    r_hat = r * jnp.exp(lc)
    b_hat = b * e_inv
    k_hat = k * e_inv
    b_til = b * e_tail
    k_til = k * e_tail
    d_last = jnp.exp(lc_last)

    even = lax.broadcasted_iota(jnp.int32, (L, LANES), 1) < HEAD_DIM

    def cat(x):
        x = x.astype(BF16)
        zero = jnp.zeros_like(x)
        return jnp.concatenate([jnp.where(even, x, zero), jnp.where(even, zero, x)], axis=0)

    def stack(*xs):
        return jnp.concatenate([x.astype(BF16) for x in xs], axis=0)

    ti = lax.broadcasted_iota(jnp.int32, (L, 2 * L), 0)
    tj = lax.broadcasted_iota(jnp.int32, (L, 2 * L), 1) % L
    strict = ti > tj
    incl = ti >= tj
    eye = jnp.where(ti == tj, 1.0, 0.0)
    pi = lax.broadcasted_iota(jnp.int32, (LANES, LANES), 0)
    qi = lax.broadcasted_iota(jnp.int32, (LANES, LANES), 1)
    same_head = (pi // HEAD_DIM) == (qi // HEAD_DIM)

    pairs = range(n_pairs)
    sls = [slice(p * LANES, (p + 1) * LANES) for p in pairs]
    v_c = [cat(v[:, sl]) for sl in sls]
    gram = [_dot_nt(stack(a_hat[:, sl], r_hat[:, sl]),
                    jnp.concatenate([cat(b_hat[:, sl]), cat(k_hat[:, sl])], axis=0))
            for sl in sls]
    nmat = [jnp.where(strict, gram[p][:L, :LANES], 0.0) for p in pairs]
    a_ak = [jnp.where(strict, gram[p][:L, LANES:], 0.0) for p in pairs]
    a_rb = [jnp.where(incl, gram[p][L:, :LANES], 0.0) for p in pairs]
    a_rk = [jnp.where(incl, gram[p][L:, LANES:], 0.0) for p in pairs]
    av = [_dot(stack(a_ak[p], a_rk[p]), v_c[p]) for p in pairs]
    tinv = [eye + nmat[p] for p in pairs]
    npow = [_dot(nmat[p], cat(nmat[p])) for p in pairs]
    span = 4
    while span < L:
        prod = [_dot(stack(tinv[p], npow[p]), cat(npow[p])) for p in pairs]
        tinv = [tinv[p] + prod[p][:L] for p in pairs]
        npow = [prod[p][L:] for p in pairs]
        span *= 2
    tinv = [tinv[p] + _dot(tinv[p], cat(npow[p])) for p in pairs]
    wu = [_dot(tinv[p], jnp.concatenate([cat(a_hat[:, sls[p]]), cat(av[p][:L])], axis=1))
          for p in pairs]
    qy = [_dot(a_rb[p], jnp.concatenate([cat(wu[p][:, :LANES]), cat(wu[p][:, LANES:])], axis=1))
          for p in pairs]
    mc = [_dot_tn(stack(b_til[:, sls[p]], k_til[:, sls[p]]),
                  jnp.concatenate([wu[p].astype(BF16),
                                   jnp.concatenate([jnp.zeros((L, LANES), BF16),
                                                    v[:, sls[p]].astype(BF16)], axis=1)], axis=0))
          for p in pairs]
    ys = []
    for p in pairs:
        dl = jnp.broadcast_to(d_last[:, sls[p]], (LANES, LANES))
        m_mat = jnp.where(same_head, mc[p][:, :LANES], 0.0) + jnp.where(pi == qi, dl, 0.0)
        c_mat = jnp.where(same_head, mc[p][:, LANES:], 0.0)
        q_t = r_hat[:, sls[p]] + qy[p][:, :LANES]
        out = _dot(stack(q_t, m_mat), s_scr[p])
        ys.append(out[:L] + qy[p][:, LANES:] + av[p][L:])
        s_scr[p] = out[L:] + c_mat

    y = jnp.concatenate(ys, axis=1)
    vec = vec_ref[...]
    ones = _head_ones()
    inv_n = 1.0 / HEAD_DIM
    mu = _head_sum(y, ones) * inv_n
    dy = y - mu
    var = _head_sum(dy * dy, ones) * inv_n
    gn = dy * lax.rsqrt(var + RWKV_GN_EPS) * vec[1:2] + vec[2:3]
    bonus = _head_sum(r * k * vec[0:1], ones) * v
    y_ref[...] = ((gn + bonus) * g_ref[...]).astype(BF16)


def _wkv(r, k, v, a, b, lw, g, vec):
    bsz, s, d = r.shape
    L = WKV_CHUNK
    spec = pl.BlockSpec((None, L, d), lambda bb, c: (bb, c, 0))
    return pl.pallas_call(
        _wkv_kernel,
        grid=(bsz, s // L),
        in_specs=[spec] * 7 + [pl.BlockSpec((SUBLANES, d), lambda bb, c: (0, 0))],
        out_specs=spec,
        out_shape=jax.ShapeDtypeStruct((bsz, s, d), BF16),
        scratch_shapes=[pltpu.VMEM((d // LANES, LANES, LANES), F32)],
        compiler_params=_cparams(("arbitrary", "arbitrary")),
        name="wkv7",
    )(r, k, v, a, b, lw, g, vec)


def _interleave(ref, scr):
    dil, n, w = ref.shape
    if dil == 1:
        return ref[0]
    for r in range(dil):
        for c in range(w // LANES):
            scr[c, pl.ds(r, n, stride=dil), :] = ref[r, :, c * LANES:(c + 1) * LANES]
    return jnp.concatenate([scr[c] for c in range(w // LANES)], axis=1)


def _merge_kernel(x_ref, ya_ref, yc_ref, o0_ref, o1_ref, o2_ref, l0_ref, l1_ref, l2_ref,
                  ga_ref, gb_ref, gc_ref, gt_ref, pa_ref, pb_ref, pc_ref, wo_ref, ln_ref, out_ref,
                  so1, so2, sl1, sl2):
    l0 = l0_ref[0]
    l1 = _interleave(l1_ref, sl1)
    l2 = _interleave(l2_ref, sl2)
    o0 = o0_ref[0]
    o1 = _interleave(o1_ref, so1)
    o2 = _interleave(o2_ref, so2)
    lm = jnp.maximum(jnp.maximum(l0, l1), l2)
    e0 = jnp.exp(l0 - lm)
    e1 = jnp.exp(l1 - lm)
    e2 = jnp.exp(l2 - lm)
    yb = (e0 * o0 + e1 * o1 + e2 * o2) / (e0 + e1 + e2)
    merged = (jax.nn.sigmoid(ga_ref[...]) * jnp.dot(ya_ref[...], pa_ref[...], preferred_element_type=F32)
              + jax.nn.sigmoid(gb_ref[...]) * _dot(yb, pb_ref[...])
              + jax.nn.sigmoid(gc_ref[...]) * jnp.dot(yc_ref[...], pc_ref[...], preferred_element_type=F32))
    y = _dot(merged, wo_ref[...])
    ln = ln_ref[...]
    out_ref[...] = _layer_norm(ALPHA * x_ref[...] + (1.0 + gt_ref[...]) * y, ln[0:1], ln[1:2])


def _merge(x, z, ya, yc, att, gt, pa, pb, pc, wo, ln):
    bsz, s, d = x.shape
    tm = 256
    wa = D_ATTN_OUT
    row = lambda w, off=0: pl.BlockSpec((None, tm, w), lambda b, i: (b, i, off // w))
    const = lambda shp: pl.BlockSpec(shp, lambda b, i: (0,) * len(shp))
    res = lambda dil: pl.BlockSpec((None, dil, tm // dil, wa), lambda b, i: (b, 0, i, 0))
    (o0, l0), (o1, l1), (o2, l2) = att
    att_specs = [res(dil) for _, dil in ATTN_GROUPS] * 2
    return pl.pallas_call(
        _merge_kernel,
        grid=(bsz, s // tm),
        scratch_shapes=[pltpu.VMEM((wa // LANES, tm, LANES), F32)] * 4,
        in_specs=[row(d), row(d), row(d)] + att_specs
                 + [row(d, OFF_GATE), row(d, OFF_GATE + d), row(d, OFF_GATE + 2 * d),
                    pl.BlockSpec((None, 1, d), lambda b, i: (b, 0, 0)),
                    const((d, d)), const((wa, d)), const((d, d)), const((d, d)),
                    const((SUBLANES, d))],
        out_specs=row(d),
        out_shape=jax.ShapeDtypeStruct((bsz, s, d), F32),
        compiler_params=_cparams(("arbitrary", "arbitrary")),
        name="merge_proj_ln",
    )(x, ya, yc, o0, o1, o2, l0, l1, l2, z, z, z, gt, pa, pb, pc, wo, ln)


def _ffn_kernel(x_ref, halo_ref, sc_ref, sh_ref, gt_ref, wu_ref, cw_ref, wd_ref, ln_ref, out_ref):
    i = pl.program_id(1)
    hr = halo_ref.shape[0]
    sc = 1.0 + sc_ref[...]
    sh = sh_ref[...]
    x = x_ref[...]
    hh = jnp.where(i == 0, 0.0, halo_ref[...] * sc + sh)
    h = jnp.concatenate([hh.astype(BF16), (x * sc + sh).astype(BF16)], axis=0)

    def up(c):
        lo, n = FFN_SLABS[c]
        return [jnp.dot(h, wu_ref[:, off:off + n], preferred_element_type=F32)
                for off in (lo, D_FF + lo)]

    def conv(u, off):
        cw = cw_ref[:, off:off + u.shape[1]]
        out = cw[CONV_F:CONV_F + 1] + cw[0:1] * u[hr:]
        for j in range(1, CONV_F):
            out = out + cw[j:j + 1] * pltpu.roll(u, j, 0)[hr:]
        return out

    acc = None
    u_next = up(0)
    for c, (lo, n) in enumerate(FFN_SLABS):
        ug, uv = u_next
        if c + 1 < len(FFN_SLABS):
            u_next = up(c + 1)
        cg = conv(ug, lo)
        cv = conv(uv, D_FF + lo)
        act = (cg * _sigmoid(cg) * cv).astype(BF16)
        part = jnp.dot(act, wd_ref[lo:lo + n, :], preferred_element_type=F32)
        acc = part if acc is None else acc + part
    ln = ln_ref[...]
    out_ref[...] = _layer_norm(ALPHA * x + (1.0 + gt_ref[...]) * acc, ln[0:1], ln[1:2])


def _conv_ffn(x, sc, sh, gt, w_up, conv_wb, w_down, ln):
    bsz, s, d = x.shape
    tm = 512
    hr = BF16_ROWS
    mod = pl.BlockSpec((None, 1, d), lambda b, i: (b, 0, 0))
    once = pl.Buffered(1)
    return pl.pallas_call(
        _ffn_kernel,
        grid=(bsz, s // tm),
        in_specs=[pl.BlockSpec((None, tm, d), lambda b, i: (b, i, 0)),
                  pl.BlockSpec((None, hr, d), lambda b, i: (b, jnp.maximum(i * (tm // hr) - 1, 0), 0)),
                  mod, mod, mod,
                  pl.BlockSpec(w_up.shape, lambda b, i: (0, 0), pipeline_mode=once),
                  pl.BlockSpec(conv_wb.shape, lambda b, i: (0, 0), pipeline_mode=once),
                  pl.BlockSpec(w_down.shape, lambda b, i: (0, 0), pipeline_mode=once),
                  pl.BlockSpec((SUBLANES, d), lambda b, i: (0, 0))],
        out_specs=pl.BlockSpec((None, tm, d), lambda b, i: (b, i, 0)),
        out_shape=jax.ShapeDtypeStruct((bsz, s, d), F32),
        compiler_params=_cparams(("arbitrary", "arbitrary")),
        name="conv_ffn_ln",
    )(x, x, sc, sh, gt, w_up, conv_wb, w_down, ln)


def _rows(vectors, n_rows=SUBLANES):
    m = jnp.stack([v.astype(F32) for v in vectors], axis=0)
    return jnp.pad(m, ((0, n_rows - m.shape[0]), (0, 0)))


def _block_diag(w, per):
    g, n, _ = w.shape
    w = w.reshape(g // per, per, n, n)
    eye = jnp.eye(per, dtype=w.dtype)
    out = jnp.einsum('cpij,pq->cpiqj', w, eye)
    return out.reshape(g // per, per * n, per * n)


def kernel(x, c, positions, mod_w, mod_b, w_in, w_in_vres, conv_a_w, conv_a_b, lru_wa, lru_ba, lru_wx, lru_bx, lru_lambda, rwkv_mu, mu_vres, w0, w2, a0, a2, g2, v0, v2, k_k, k_a, r_k, ln_x_w, ln_x_b, proj_a, proj_b, proj_c, w_o, ln1_w, ln1_b, ffn_up, ffn_conv_w, ffn_conv_b, ffn_down, ln2_w, ln2_b):
    bsz, s, d = x.shape
    mod = _modulation(c, mod_w, mod_b)
    half = ROT_DIM // 2
    inv_freq = ROPE_THETA ** (-jnp.arange(half, dtype=F32) / half)
    lane = jnp.arange(LANES)
    freq_lane = jnp.where((lane % HEAD_DIM) < ROT_DIM, inv_freq[lane % half], 0.0).reshape(1, LANES)
    rope_tabs = _rope_tables(positions.reshape(bsz, s, 1), freq_lane)
    zero_d = jnp.zeros((d,), F32)

    v_first = None
    for l in range(DEPTH):
        m6 = mod[l].reshape(bsz, 6, 1, d)
        sh1, sc1, gt1, sh2, sc2, gt2 = (m6[:, i] for i in range(6))

        wl = w_in[l]
        w_main = jnp.concatenate([wl[:, 0:2048], wl[:, 4352:7424], wl[:, 7680:10752],
                                  wl[:, 7424:7680]], axis=1).astype(BF16)
        if l == 0:
            w_vres = None
        else:
            w_vres = jnp.pad(w_in_vres[l - 1], ((0, 0), (0, LANES - MV_LORA))).astype(BF16)
        z = _main_projection(x, sc1, sh1, w_main)
        qkv_groups, zv1 = _attn_projection(x, sc1, sh1, wl[:, 2048:4352].astype(BF16), w_vres,
                                           rope_tabs)

        per = 256 // RNN_BLOCK
        y_a = _rglru(z, conv_a_w[l],
                     _rows([conv_a_b[l], lru_ba[l], lru_bx[l], lru_lambda[l]]),
                     _block_diag(lru_wa[l], per).astype(BF16),
                     _block_diag(lru_wx[l], per).astype(BF16))

        att = [_attention_group(qkv_groups[g], g) for g in range(N_GROUPS)]

        mu = rwkv_mu[l]
        vec = _rows([mu[0:1024], mu[1024:2048], mu[2048:3072], w0[l], a0[l], k_k[l], k_a[l],
                     v0[l - 1] if l > 0 else zero_d])
        mu_l = mu[3072:3328].reshape(1, 2 * LANES)
        w2p = jnp.pad(w2[l], ((0, LANES - LORA_W), (0, 0))).astype(BF16)
        a2p = jnp.pad(a2[l], ((LORA_W, 0), (0, 0))).astype(BF16)
        if l == 0:
            prep = _rwkv_prep(z, None, None, vec, mu_l, w2p, a2p, g2[l].astype(BF16), None, None)
        else:
            mu_v1 = jnp.pad(mu_vres[l - 1], (0, LANES - MV_LORA)).reshape(1, LANES)
            v2p = jnp.pad(v2[l - 1], ((0, LANES - MV_LORA), (0, 0))).astype(BF16)
            prep = _rwkv_prep(z, zv1, v_first, vec, mu_l, w2p, a2p, g2[l].astype(BF16), mu_v1, v2p)
        r_, k_, v_, a_, b_, lw_, g_ = prep
        if l == 0:
            v_first = v_
        y_c = _wkv(r_, k_, v_, a_, b_, lw_, g_,
                   _rows([r_k[l].reshape(-1), ln_x_w[l], ln_x_b[l]]))

        x = _merge(x, z, y_a, y_c, att, gt1, proj_a[l].astype(BF16), proj_b[l].astype(BF16),
                   proj_c[l].astype(BF16), w_o[l].astype(BF16), _rows([ln1_w[l], ln1_b[l]]))

        conv_wb = jnp.concatenate([ffn_conv_w[l], ffn_conv_b[l][None, :]], axis=0)
        conv_wb = jnp.pad(conv_wb, ((0, SUBLANES - conv_wb.shape[0]), (0, 0)))
        x = _conv_ffn(x, sc2, sh2, gt2, ffn_up[l].astype(BF16), conv_wb,
                      ffn_down[l].astype(BF16), _rows([ln2_w[l], ln2_b[l]]))
    return x
```

```python
import functools

import jax
import jax.numpy as jnp
from jax import lax
from jax.experimental import pallas as pl
from jax.experimental.pallas import tpu as pltpu

F32 = jnp.float32
BF16 = jnp.bfloat16

D_MODEL = 1024
DEPTH = 2
D_RNN = 1024
RNN_BLOCK = 64
CONV_A = 4
LRU_C = 8.0
HEAD_DIM = 64
ATTN_GROUPS = ((128, 1), (512, 4), (2048, 16))
HEADS_PER_GROUP = 4
N_GROUPS = 3
D_ATTN = 768
D_ATTN_OUT = 256
ROT_DIM = 16
ROPE_THETA = 500000.0
D_RWKV = 1024
LORA_W = 64
LORA_A = 64
LORA_G = 128
MV_LORA = 32
RWKV_GN_EPS = 64e-5
DECAY_SCALE = 0.6065306597126334
D_FF = 2816
CONV_F = 3
ALPHA = (2 * DEPTH) ** 0.25
LN_EPS = 1e-5

LANES = 128
SUBLANES = 8
BF16_ROWS = 16

OFF_XA = 0
OFF_GA = 1024
OFF_R = 2048
OFF_K = 3072
OFF_V = 4096
OFF_GATE = 5120
OFF_LORA = 8192
N_Z = 8448
MAIN_TILE = 2816

FFN_SLABS = ((0, 1536), (1536, 1280))
WKV_CHUNK = 64
RWKV_TILE = 256
ATTN_BLOCK = 128
ATTN_PER_STEP = 2
VMEM_LIMIT = 52 * 1024 * 1024


def _cparams(sem):
    return pltpu.CompilerParams(dimension_semantics=sem, vmem_limit_bytes=VMEM_LIMIT)


def _dot(a, b):
    return jnp.dot(a.astype(BF16), b.astype(BF16), preferred_element_type=F32)


def _dot_nt(a, b):
    return lax.dot_general(a.astype(BF16), b.astype(BF16), (((1,), (1,)), ((), ())),
                           preferred_element_type=F32)


def _dot_tn(a, b):
    return lax.dot_general(a.astype(BF16), b.astype(BF16), (((0,), (0,)), ((), ())),
                           preferred_element_type=F32)


def _split3(x):
    p0 = x.astype(BF16)
    r1 = x - p0.astype(F32)
    p1 = r1.astype(BF16)
    p2 = (r1 - p1.astype(F32)).astype(BF16)
    return p0, p1, p2


def _head_ones():
    i = lax.broadcasted_iota(jnp.int32, (LANES, LANES), 0) // HEAD_DIM
    j = lax.broadcasted_iota(jnp.int32, (LANES, LANES), 1) // HEAD_DIM
    return jnp.where(i == j, 1.0, 0.0).astype(BF16)


def _head_sum(x, ones):
    rows = x.shape[0]
    n = x.shape[1] // LANES
    xs = jnp.concatenate([x[:, c * LANES:(c + 1) * LANES] for c in range(n)], axis=0)
    hi = xs.astype(BF16)
    lo = (xs - hi.astype(F32)).astype(BF16)
    out = jnp.dot(jnp.concatenate([hi, lo], axis=0), ones, preferred_element_type=F32)
    tot = out[:n * rows] + out[n * rows:]
    return jnp.concatenate([tot[c * rows:(c + 1) * rows] for c in range(n)], axis=1)


def _softplus(x):
    return jnp.maximum(x, 0.0) + jnp.log1p(jnp.exp(-jnp.abs(x)))


def _sigmoid(x):
    return 0.5 * jnp.tanh(0.5 * x) + 0.5


def _layer_norm(x, w, b):
    mu = jnp.mean(x, axis=-1, keepdims=True)
    d = x - mu
    var = jnp.mean(d * d, axis=-1, keepdims=True)
    return d * lax.rsqrt(var + LN_EPS) * w + b


def _mod_kernel(c_ref, w_ref, b_ref, o_ref):
    c = c_ref[...]
    s = c * jax.nn.sigmoid(c)
    s0, s1, s2 = _split3(s)
    w0, w1, w2 = _split3(w_ref[...])
    acc = jnp.dot(s0, w0, preferred_element_type=F32)
    acc += jnp.dot(s0, w1, preferred_element_type=F32) + jnp.dot(s1, w0, preferred_element_type=F32)
    acc += (jnp.dot(s1, w1, preferred_element_type=F32) + jnp.dot(s0, w2, preferred_element_type=F32)
            + jnp.dot(s2, w0, preferred_element_type=F32))
    o_ref[...] = acc + b_ref[...]


def _modulation(c, mod_w, mod_b):
    depth, d, n = mod_w.shape
    bsz = c.shape[0]
    rows = BF16_ROWS
    tn = 1536
    c_pad = jnp.pad(c, ((0, rows - bsz), (0, 0)))
    out = pl.pallas_call(
        _mod_kernel,
        grid=(depth, n // tn),
        in_specs=[pl.BlockSpec((rows, d), lambda l, j: (0, 0)),
                  pl.BlockSpec((None, d, tn), lambda l, j: (l, 0, j)),
                  pl.BlockSpec((None, 1, tn), lambda l, j: (l, 0, j))],
        out_specs=pl.BlockSpec((None, rows, tn), lambda l, j: (l, 0, j)),
        out_shape=jax.ShapeDtypeStruct((depth, rows, n), F32),
        compiler_params=_cparams(("arbitrary", "arbitrary")),
        name="adaln_mod",
    )(c_pad, mod_w, mod_b.reshape(depth, 1, n))
    return out[:, :bsz]


def _main_proj_kernel(x_ref, sc_ref, sh_ref, w_ref, z_ref):
    h = (x_ref[...] * (1.0 + sc_ref[...]) + sh_ref[...]).astype(BF16)
    z_ref[...] = jnp.dot(h, w_ref[...], preferred_element_type=F32)


def _main_projection(x, sc, sh, w):
    bsz, s, d = x.shape
    n = w.shape[1]
    tm, tn = 512, MAIN_TILE
    return pl.pallas_call(
        _main_proj_kernel,
        grid=(n // tn, bsz, s // tm),
        in_specs=[pl.BlockSpec((None, tm, d), lambda j, b, i: (b, i, 0)),
                  pl.BlockSpec((None, 1, d), lambda j, b, i: (b, 0, 0)),
                  pl.BlockSpec((None, 1, d), lambda j, b, i: (b, 0, 0)),
                  pl.BlockSpec((d, tn), lambda j, b, i: (0, j))],
        out_specs=pl.BlockSpec((None, tm, tn), lambda j, b, i: (b, i, j)),
        out_shape=jax.ShapeDtypeStruct((bsz, s, n), F32),
        compiler_params=_cparams(("arbitrary", "arbitrary", "arbitrary")),
        name="main_projection",
    )(x, sc, sh, w)


def _attn_proj_kernel(x_ref, sc_ref, sh_ref, w_ref, cos_ref, sa_ref, sb_ref, *rest, has_vres):
    if has_vres:
        wv_ref, g0_ref, g1_ref, g2_ref, zv_ref, r_scr = rest
    else:
        g0_ref, g1_ref, g2_ref, r_scr = rest
    tm = x_ref.shape[0]
    width = HEADS_PER_GROUP * HEAD_DIM
    half = ROT_DIM // 2
    h = (x_ref[...] * (1.0 + sc_ref[...]) + sh_ref[...]).astype(BF16)
    if has_vres:
        zv_ref[...] = jnp.dot(h, wv_ref[...], preferred_element_type=F32)
    acc = jnp.dot(h, w_ref[...], preferred_element_type=F32)
    n_slab = D_ATTN // LANES
    for sec in range(3):
        scale = HEAD_DIM ** -0.5 if sec == 0 else 1.0
        for c in range(n_slab):
            lo = sec * D_ATTN + c * LANES
            t = acc[:, lo:lo + LANES]
            if sec < 2:
                t = (t * cos_ref[...] + pltpu.roll(t, half, 1) * sa_ref[...]
                     + pltpu.roll(t, LANES - half, 1) * sb_ref[...]) * scale
            r_scr[sec * n_slab + c] = t
    per = width // LANES
    for sec in range(3):
        for g, g_ref in enumerate((g0_ref, g1_ref, g2_ref)):
            dil = g_ref.shape[0]
            for r in range(dil):
                rows = pl.ds(r, tm // dil, stride=dil) if dil > 1 else slice(None)
                for c in range(per):
                    lo = sec * width + c * LANES
                    g_ref[r, :, lo:lo + LANES] = r_scr[sec * n_slab + g * per + c, rows, :].astype(BF16)


def _attn_projection(x, sc, sh, w, w_vres, rope_tabs):
    bsz, s, d = x.shape
    n = w.shape[1]
    tm = 512
    has_vres = w_vres is not None
    tab = pl.BlockSpec((None, tm, LANES), lambda b, i: (b, i, 0))
    in_specs = [pl.BlockSpec((None, tm, d), lambda b, i: (b, i, 0)),
                pl.BlockSpec((None, 1, d), lambda b, i: (b, 0, 0)),
                pl.BlockSpec((None, 1, d), lambda b, i: (b, 0, 0)),
                pl.BlockSpec((d, n), lambda b, i: (0, 0)),
                tab, tab, tab]
    out_specs, out_shape = [], []
    for _, dil in ATTN_GROUPS:
        out_specs.append(pl.BlockSpec((None, dil, tm // dil, D_ATTN), lambda b, i: (b, 0, i, 0)))
        out_shape.append(jax.ShapeDtypeStruct((bsz, dil, s // dil, D_ATTN), BF16))
    args = [x, sc, sh, w, *rope_tabs]
    if has_vres:
        in_specs.append(pl.BlockSpec((d, LANES), lambda b, i: (0, 0)))
        out_specs.append(pl.BlockSpec((None, tm, LANES), lambda b, i: (b, i, 0)))
        out_shape.append(jax.ShapeDtypeStruct((bsz, s, LANES), F32))
        args.append(w_vres)
    outs = pl.pallas_call(
        functools.partial(_attn_proj_kernel, has_vres=has_vres),
        grid=(bsz, s // tm),
        in_specs=in_specs,
        out_specs=out_specs,
        out_shape=out_shape,
        scratch_shapes=[pltpu.VMEM((n // LANES, tm, LANES), F32)],
        compiler_params=_cparams(("arbitrary", "arbitrary")),
        name="attn_projection",
    )(*args)
    return outs[0:3], (outs[3] if has_vres else None)


def _rglru_kernel(xa_ref, ga_ref, cw_ref, vec_ref, wa_ref, wx_ref, y_ref, xbuf, hc):
    t = pl.program_id(2)
    tt, cwid = xa_ref.shape

    @pl.when(t == 0)
    def _():
        xbuf[0:SUBLANES, :] = jnp.zeros((SUBLANES, cwid), F32)
        hc[...] = jnp.zeros_like(hc)

    xa = xa_ref[...]
    xbuf[SUBLANES:SUBLANES + tt, :] = xa
    cw = cw_ref[...]
    vec = vec_ref[...]
    xc = vec[0:1] + cw[0:1] * xa
    for j in range(1, CONV_A):
        xc = xc + cw[j:j + 1] * xbuf[SUBLANES - j:SUBLANES - j + tt, :]
    xbuf[0:SUBLANES, :] = xbuf[tt:tt + SUBLANES, :]

    xb = xc.astype(BF16)
    r = jax.nn.sigmoid(jnp.dot(xb, wa_ref[...], preferred_element_type=F32) + vec[1:2])
    i = jax.nn.sigmoid(jnp.dot(xb, wx_ref[...], preferred_element_type=F32) + vec[2:3])
    log_a = (-LRU_C) * r * _softplus(-vec[3:4])
    a = jnp.exp(log_a)
    b = jnp.sqrt(-jnp.tanh(log_a) * (a * a + 1.0)) * (i * xc)

    ng = tt // SUBLANES
    a = a.reshape(ng, SUBLANES, cwid)
    b = b.reshape(ng, SUBLANES, cwid)
    rows = lax.broadcasted_iota(jnp.int32, (ng, SUBLANES, cwid), 1)
    step = 1
    while step < SUBLANES:
        a_sh = pltpu.roll(a, step, 1)
        b_sh = pltpu.roll(b, step, 1)
        m = rows >= step
        b = jnp.where(m, a * b_sh + b, b)
        a = jnp.where(m, a * a_sh, a)
        step *= 2
    g = ga_ref[...]
    gelu = 0.5 * g * (1.0 + jnp.tanh(0.7978845608028654 * (g + 0.044715 * (g * g * g))))
    carry = hc[0:1, :]
    hs = []
    for i in range(ng):
        h = a[i] * carry + b[i]
        carry = h[SUBLANES - 1:SUBLANES, :]
        hs.append(h)
    hc[0:1, :] = carry
    y_ref[...] = (jnp.concatenate(hs, axis=0) * gelu).astype(BF16)


def _rglru(z, conv_w, vec, wa_bd, wx_bd):
    bsz, s, _ = z.shape
    tt, cwid = 256, 256
    nc = D_RNN // cwid
    return pl.pallas_call(
        _rglru_kernel,
        grid=(bsz, nc, s // tt),
        in_specs=[pl.BlockSpec((None, tt, cwid), lambda b, c, t: (b, t, OFF_XA // cwid + c)),
                  pl.BlockSpec((None, tt, cwid), lambda b, c, t: (b, t, OFF_GA // cwid + c)),
                  pl.BlockSpec((CONV_A, cwid), lambda b, c, t: (0, c)),
                  pl.BlockSpec((SUBLANES, cwid), lambda b, c, t: (0, c)),
                  pl.BlockSpec((None, cwid, cwid), lambda b, c, t: (c, 0, 0)),
                  pl.BlockSpec((None, cwid, cwid), lambda b, c, t: (c, 0, 0))],
        out_specs=pl.BlockSpec((None, tt, cwid), lambda b, c, t: (b, t, c)),
        out_shape=jax.ShapeDtypeStruct((bsz, s, D_RNN), BF16),
        scratch_shapes=[pltpu.VMEM((tt + SUBLANES, cwid), F32), pltpu.VMEM((SUBLANES, cwid), F32)],
        compiler_params=_cparams(("arbitrary", "arbitrary", "arbitrary")),
        name="rglru",
    )(z, z, conv_w, vec, wa_bd, wx_bd)


def _rope_table_kernel(pos_ref, freq_ref, cos_ref, sa_ref, sb_ref):
    tt = pos_ref.shape[0]
    ang = pos_ref[...].astype(F32) * freq_ref[...]
    lane = lax.broadcasted_iota(jnp.int32, (tt, LANES), 1) % HEAD_DIM
    sn = jnp.sin(ang)
    half = ROT_DIM // 2
    cos_ref[...] = jnp.cos(ang)
    sa_ref[...] = jnp.where((lane >= half) & (lane < ROT_DIM), sn, 0.0)
    sb_ref[...] = jnp.where(lane < half, -sn, 0.0)


def _rope_tables(pos3, freq_lane):
    bsz, s, _ = pos3.shape
    tt = 1024
    out = pl.BlockSpec((None, tt, LANES), lambda b, t: (b, t, 0))
    return pl.pallas_call(
        _rope_table_kernel,
        grid=(bsz, s // tt),
        in_specs=[pl.BlockSpec((None, tt, 1), lambda b, t: (b, t, 0)),
                  pl.BlockSpec((1, LANES), lambda b, t: (0, 0))],
        out_specs=[out] * 3,
        out_shape=[jax.ShapeDtypeStruct((bsz, s, LANES), F32)] * 3,
        compiler_params=_cparams(("arbitrary", "arbitrary")),
        name="rope_tables",
    )(pos3, freq_lane)


def _attn_kernel(q_ref, kp_ref, kc_ref, vp_ref, vc_ref, o_ref, lse_ref):
    nq = ATTN_BLOCK
    width = q_ref.shape[1]
    nh = HEADS_PER_GROUP
    nk = 2 * nq
    k_all = jnp.concatenate([kp_ref[...], kc_ref[...]], axis=0)
    v_all = jnp.concatenate([vp_ref[...], vc_ref[...]], axis=0)
    head = lax.broadcasted_iota(jnp.int32, (nq, width), 1) // HEAD_DIM
    qi = lax.broadcasted_iota(jnp.int32, (nh * nq, nk), 0) % nq
    kj = lax.broadcasted_iota(jnp.int32, (nh * nq, nk), 1)
    band = (kj >= qi) & (kj <= qi + nq)
    for j in range(ATTN_PER_STEP):
        q = q_ref[j * nq:(j + 1) * nq, :]
        k = k_all[j * nq:j * nq + nk]
        v = v_all[j * nq:j * nq + nk]
        qcat = jnp.concatenate([jnp.where(head == h, q, jnp.zeros_like(q)) for h in range(nh)],
                               axis=0)
        s = lax.dot_general(qcat, k, (((1,), (1,)), ((), ())), preferred_element_type=F32)
        if j == 0:
            valid = band & ((pl.program_id(2) > 0) | (kj >= nq))
        else:
            valid = band
        s = jnp.where(valid, s, -1e30)
        m = jnp.max(s, axis=1, keepdims=True)
        p = jnp.exp(s - m)
        l = jnp.sum(p, axis=1, keepdims=True)
        pv = jnp.dot(p.astype(BF16), v, preferred_element_type=F32) / l
        lse_rows = m + jnp.log(l)
        o = jnp.zeros((nq, width), F32)
        lse = jnp.zeros((nq, width), F32)
        for h in range(nh):
            hm = head == h
            rows = slice(h * nq, (h + 1) * nq)
            o = jnp.where(hm, pv[rows], o)
            lse = jnp.where(hm, lse_rows[rows], lse)
        o_ref[j * nq:(j + 1) * nq, :] = o
        lse_ref[j * nq:(j + 1) * nq, :] = lse


def _attention_group(qkv, g):
    bsz, dil, sub, _ = qkv.shape
    width = HEADS_PER_GROUP * HEAD_DIM
    nb = ATTN_BLOCK

    per = ATTN_PER_STEP

    def cur(sec):
        return lambda b, r, m: (b, r, m, sec)

    def prev(sec):
        return lambda b, r, m: (b, r, jnp.maximum(m * per - 1, 0), sec)

    blk = (None, None, per * nb, width)
    pblk = (None, None, nb, width)
    return pl.pallas_call(
        _attn_kernel,
        grid=(bsz, dil, sub // (per * nb)),
        in_specs=[pl.BlockSpec(blk, cur(0)),
                  pl.BlockSpec(pblk, prev(1)), pl.BlockSpec(blk, cur(1)),
                  pl.BlockSpec(pblk, prev(2)), pl.BlockSpec(blk, cur(2))],
        out_specs=[pl.BlockSpec(blk, cur(0)), pl.BlockSpec(blk, cur(0))],
        out_shape=[jax.ShapeDtypeStruct((bsz, dil, sub, width), F32)] * 2,
        compiler_params=_cparams(("arbitrary", "arbitrary", "arbitrary")),
        name=f"dilated_attention_g{g}",
    )(qkv, qkv, qkv, qkv, qkv)


def _token_shift(z, halo, first, mu):
    ext = jnp.concatenate([jnp.where(first, 0.0, halo), z], axis=0)
    zp = pltpu.roll(ext, 1, 0)[SUBLANES:]
    return z + (zp - z) * mu


def _wkv_chunk(r, k, v, a, b, lw, s_scr):
    L = r.shape[0]
    n_pairs = r.shape[1] // LANES
    ti = lax.broadcasted_iota(jnp.int32, (L, L), 0)
    tj = lax.broadcasted_iota(jnp.int32, (L, L), 1)
    tri = jnp.where(ti >= tj, 1.0, 0.0).astype(BF16)
    l0, l1, l2 = _split3(lw)
    lc = (jnp.dot(tri, l0, preferred_element_type=F32) + jnp.dot(tri, l1, preferred_element_type=F32)
          + jnp.dot(tri, l2, preferred_element_type=F32))
    lc_last = lc[L - 1:L, :]
    e_inv = jnp.exp(-lc)
    e_tail = jnp.exp(lc_last - lc)
    a_hat = a * jnp.exp(lc - lw)
    r_hat = r * jnp.exp(lc)
    b_hat = b * e_inv
    k_hat = k * e_inv
    b_til = b * e_tail
    k_til = k * e_tail
    d_last = jnp.exp(lc_last)

    even = lax.broadcasted_iota(jnp.int32, (L, LANES), 1) < HEAD_DIM

    def cat(x):
        x = x.astype(BF16)
        zero = jnp.zeros_like(x)
        return jnp.concatenate([jnp.where(even, x, zero), jnp.where(even, zero, x)], axis=0)

    def stack(*xs):
        return jnp.concatenate([x.astype(BF16) for x in xs], axis=0)

    ti = lax.broadcasted_iota(jnp.int32, (L, 2 * L), 0)
    tj = lax.broadcasted_iota(jnp.int32, (L, 2 * L), 1) % L
    strict = ti > tj
    incl = ti >= tj
    eye = jnp.where(ti == tj, 1.0, 0.0)
    pi = lax.broadcasted_iota(jnp.int32, (LANES, LANES), 0)
    qi = lax.broadcasted_iota(jnp.int32, (LANES, LANES), 1)
    same_head = (pi // HEAD_DIM) == (qi // HEAD_DIM)

    pairs = range(n_pairs)
    sls = [slice(p * LANES, (p + 1) * LANES) for p in pairs]
    v_c = [cat(v[:, sl]) for sl in sls]
    gram = [_dot_nt(stack(a_hat[:, sl], r_hat[:, sl]),
                    jnp.concatenate([cat(b_hat[:, sl]), cat(k_hat[:, sl])], axis=0))
            for sl in sls]
    nmat = [jnp.where(strict, gram[p][:L, :LANES], 0.0) for p in pairs]
    a_ak = [jnp.where(strict, gram[p][:L, LANES:], 0.0) for p in pairs]
    a_rb = [jnp.where(incl, gram[p][L:, :LANES], 0.0) for p in pairs]
    a_rk = [jnp.where(incl, gram[p][L:, LANES:], 0.0) for p in pairs]
    av = [_dot(stack(a_ak[p], a_rk[p]), v_c[p]) for p in pairs]
    tinv = [eye + nmat[p] for p in pairs]
    npow = [_dot(nmat[p], cat(nmat[p])) for p in pairs]
    span = 4
    while span < L:
        prod = [_dot(stack(tinv[p], npow[p]), cat(npow[p])) for p in pairs]
        tinv = [tinv[p] + prod[p][:L] for p in pairs]
        npow = [prod[p][L:] for p in pairs]
        span *= 2
    tinv = [tinv[p] + _dot(tinv[p], cat(npow[p])) for p in pairs]
    wu = [_dot(tinv[p], jnp.concatenate([cat(a_hat[:, sls[p]]), cat(av[p][:L])], axis=1))
          for p in pairs]
    qy = [_dot(a_rb[p], jnp.concatenate([cat(wu[p][:, :LANES]), cat(wu[p][:, LANES:])], axis=1))
          for p in pairs]
    mc = [_dot_tn(stack(b_til[:, sls[p]], k_til[:, sls[p]]),
                  jnp.concatenate([wu[p].astype(BF16),
                                   jnp.concatenate([jnp.zeros((L, LANES), BF16),
                                                    v[:, sls[p]].astype(BF16)], axis=1)], axis=0))
          for p in pairs]
    ys = []
    for p in pairs:
        dl = jnp.broadcast_to(d_last[:, sls[p]], (LANES, LANES))
        m_mat = jnp.where(same_head, mc[p][:, :LANES], 0.0) + jnp.where(pi == qi, dl, 0.0)
        c_mat = jnp.where(same_head, mc[p][:, LANES:], 0.0)
        q_t = r_hat[:, sls[p]] + qy[p][:, :LANES]
        out = _dot(stack(q_t, m_mat), s_scr[p])
        ys.append(out[:L] + qy[p][:, LANES:] + av[p][L:])
        s_scr[p] = out[L:] + c_mat

    return jnp.concatenate(ys, axis=1)


def _rwkv_kernel(*refs, has_vres):
    if has_vres:
        (zr_ref, zk_ref, zv_ref, zl_ref, hr_ref, hk_ref, hv_ref, hl_ref, vec_ref, mul_ref,
         w2_ref, a2_ref, g2_ref, zv1_ref, hv1_ref, vf_ref, muv_ref, v2_ref, y_ref, s_scr) = refs
    else:
        (zr_ref, zk_ref, zv_ref, zl_ref, hr_ref, hk_ref, hv_ref, hl_ref, vec_ref, mul_ref,
         w2_ref, a2_ref, g2_ref, y_ref, vf_out, s_scr) = refs
    first = pl.program_id(1) == 0

    @pl.when(first)
    def _():
        s_scr[...] = jnp.zeros_like(s_scr)

    vec = vec_ref[...]
    r = _token_shift(zr_ref[...], hr_ref[...], first, vec[0:1])
    k = _token_shift(zk_ref[...], hk_ref[...], first, vec[1:2])
    v = _token_shift(zv_ref[...], hv_ref[...], first, vec[2:3])
    lora = _token_shift(zl_ref[...], hl_ref[...], first, mul_ref[...])
    wa_in = lora[:, 0:LANES]
    w_pre = vec[3:4] + _dot(jnp.tanh(wa_in), w2_ref[...])
    lw = (-DECAY_SCALE) * _sigmoid(w_pre)
    alpha = _sigmoid(vec[4:5] + _dot(wa_in, a2_ref[...]))
    g = _dot(_sigmoid(lora[:, LANES:2 * LANES]), g2_ref[...])
    if has_vres:
        zv1 = _token_shift(zv1_ref[...], hv1_ref[...], first, muv_ref[...])
        mix = _sigmoid(vec[7:8] + _dot(zv1, v2_ref[...]))
        v = v + (vf_ref[...] - v) * mix
    else:
        vf_out[...] = v
    ones = _head_ones()
    kk = k * vec[5:6]
    kk = kk * lax.rsqrt(_head_sum(kk * kk, ones) + 1e-12)
    k = k * (1.0 + (alpha - 1.0) * vec[6:7])
    a = -kk
    b = kk * alpha

    L = WKV_CHUNK
    ys = []
    for c in range(r.shape[0] // L):
        rows = slice(c * L, (c + 1) * L)
        ys.append(_wkv_chunk(r[rows], k[rows], v[rows], a[rows], b[rows], lw[rows], s_scr))
    y = jnp.concatenate(ys, axis=0)

    inv_n = 1.0 / HEAD_DIM
    mu = _head_sum(y, ones) * inv_n
    dy = y - mu
    var = _head_sum(dy * dy, ones) * inv_n
    gn = dy * lax.rsqrt(var + RWKV_GN_EPS) * vec[9:10] + vec[10:11]
    bonus = _head_sum(r * k * vec[8:9], ones) * v
    y_ref[...] = ((gn + bonus) * g).astype(BF16)


def _rwkv(z, zv1, v_first, vec, mu_l, w2p, a2p, g2, mu_v1, v2p):
    bsz, s, _ = z.shape
    tt = RWKV_TILE
    d = D_RWKV
    has_vres = zv1 is not None
    hstep = tt // SUBLANES

    def col(off, w):
        return lambda b, t: (b, t, off // w)

    def hcol(off, w):
        return lambda b, t: (b, jnp.maximum(t * hstep - 1, 0), off // w)

    lw = 2 * LANES
    in_specs = [pl.BlockSpec((None, tt, d), col(OFF_R, d)),
                pl.BlockSpec((None, tt, d), col(OFF_K, d)),
                pl.BlockSpec((None, tt, d), col(OFF_V, d)),
                pl.BlockSpec((None, tt, lw), col(OFF_LORA, lw)),
                pl.BlockSpec((None, SUBLANES, d), hcol(OFF_R, d)),
                pl.BlockSpec((None, SUBLANES, d), hcol(OFF_K, d)),
                pl.BlockSpec((None, SUBLANES, d), hcol(OFF_V, d)),
                pl.BlockSpec((None, SUBLANES, lw), hcol(OFF_LORA, lw)),
                pl.BlockSpec(vec.shape, lambda b, t: (0, 0)),
                pl.BlockSpec((1, lw), lambda b, t: (0, 0)),
                pl.BlockSpec((LANES, d), lambda b, t: (0, 0)),
                pl.BlockSpec((LANES, d), lambda b, t: (0, 0)),
                pl.BlockSpec((LANES, d), lambda b, t: (0, 0))]
    args = [z, z, z, z, z, z, z, z, vec, mu_l, w2p, a2p, g2]
    row = pl.BlockSpec((None, tt, d), lambda b, t: (b, t, 0))
    out_specs = [row]
    out_shape = [jax.ShapeDtypeStruct((bsz, s, d), BF16)]
    if has_vres:
        in_specs += [pl.BlockSpec((None, tt, LANES), lambda b, t: (b, t, 0)),
                     pl.BlockSpec((None, SUBLANES, LANES),
                                  lambda b, t: (b, jnp.maximum(t * hstep - 1, 0), 0)),
                     row,
                     pl.BlockSpec((1, LANES), lambda b, t: (0, 0)),
                     pl.BlockSpec((LANES, d), lambda b, t: (0, 0))]
        args += [zv1, zv1, v_first, mu_v1, v2p]
    else:
        out_specs.append(row)
        out_shape.append(jax.ShapeDtypeStruct((bsz, s, d), F32))
    outs = pl.pallas_call(
        functools.partial(_rwkv_kernel, has_vres=has_vres),
        grid=(bsz, s // tt),
        in_specs=in_specs,
        out_specs=out_specs,
        out_shape=out_shape,
        scratch_shapes=[pltpu.VMEM((d // LANES, LANES, LANES), F32)],
        compiler_params=_cparams(("arbitrary", "arbitrary")),
        name="rwkv7",
    )(*args)
    return outs[0], (v_first if has_vres else outs[1])


def _interleave(ref, scr):
    dil, n, w = ref.shape
    if dil == 1:
        return ref[0]
    for r in range(dil):
        for c in range(w // LANES):
            scr[c, pl.ds(r, n, stride=dil), :] = ref[r, :, c * LANES:(c + 1) * LANES]
    return jnp.concatenate([scr[c] for c in range(w // LANES)], axis=1)


def _merge_kernel(x_ref, ya_ref, yc_ref, o0_ref, o1_ref, o2_ref, l0_ref, l1_ref, l2_ref,
                  ga_ref, gb_ref, gc_ref, gt_ref, pa_ref, pb_ref, pc_ref, wo_ref, ln_ref, out_ref,
                  so1, so2, sl1, sl2):
    l0 = l0_ref[0]
    l1 = _interleave(l1_ref, sl1)
    l2 = _interleave(l2_ref, sl2)
    o0 = o0_ref[0]
    o1 = _interleave(o1_ref, so1)
    o2 = _interleave(o2_ref, so2)
    lm = jnp.maximum(jnp.maximum(l0, l1), l2)
    e0 = jnp.exp(l0 - lm)
    e1 = jnp.exp(l1 - lm)
    e2 = jnp.exp(l2 - lm)
    yb = (e0 * o0 + e1 * o1 + e2 * o2) / (e0 + e1 + e2)
    merged = (jax.nn.sigmoid(ga_ref[...]) * jnp.dot(ya_ref[...], pa_ref[...], preferred_element_type=F32)
              + jax.nn.sigmoid(gb_ref[...]) * _dot(yb, pb_ref[...])
              + jax.nn.sigmoid(gc_ref[...]) * jnp.dot(yc_ref[...], pc_ref[...], preferred_element_type=F32))
    y = _dot(merged, wo_ref[...])
    ln = ln_ref[...]
    out_ref[...] = _layer_norm(ALPHA * x_ref[...] + (1.0 + gt_ref[...]) * y, ln[0:1], ln[1:2])


def _merge(x, z, ya, yc, att, gt, pa, pb, pc, wo, ln):
    bsz, s, d = x.shape
    tm = 256
    wa = D_ATTN_OUT
    row = lambda w, off=0: pl.BlockSpec((None, tm, w), lambda b, i: (b, i, off // w))
    const = lambda shp: pl.BlockSpec(shp, lambda b, i: (0,) * len(shp))
    res = lambda dil: pl.BlockSpec((None, dil, tm // dil, wa), lambda b, i: (b, 0, i, 0))
    (o0, l0), (o1, l1), (o2, l2) = att
    att_specs = [res(dil) for _, dil in ATTN_GROUPS] * 2
    return pl.pallas_call(
        _merge_kernel,
        grid=(bsz, s // tm),
        scratch_shapes=[pltpu.VMEM((wa // LANES, tm, LANES), F32)] * 4,
        in_specs=[row(d), row(d), row(d)] + att_specs
                 + [row(d, OFF_GATE), row(d, OFF_GATE + d), row(d, OFF_GATE + 2 * d),
                    pl.BlockSpec((None, 1, d), lambda b, i: (b, 0, 0)),
                    const((d, d)), const((wa, d)), const((d, d)), const((d, d)),
                    const((SUBLANES, d))],
        out_specs=row(d),
        out_shape=jax.ShapeDtypeStruct((bsz, s, d), F32),
        compiler_params=_cparams(("arbitrary", "arbitrary")),
        name="merge_proj_ln",
    )(x, ya, yc, o0, o1, o2, l0, l1, l2, z, z, z, gt, pa, pb, pc, wo, ln)


def _ffn_kernel(x_ref, halo_ref, sc_ref, sh_ref, gt_ref, wu_ref, cw_ref, wd_ref, ln_ref, out_ref):
    i = pl.program_id(1)
    hr = halo_ref.shape[0]
    sc = 1.0 + sc_ref[...]
    sh = sh_ref[...]
    x = x_ref[...]
    hh = jnp.where(i == 0, 0.0, halo_ref[...] * sc + sh)
    h = jnp.concatenate([hh.astype(BF16), (x * sc + sh).astype(BF16)], axis=0)

    def up(c):
        lo, n = FFN_SLABS[c]
        return [jnp.dot(h, wu_ref[:, off:off + n], preferred_element_type=F32)
                for off in (lo, D_FF + lo)]

    def conv(u, off):
        cw = cw_ref[:, off:off + u.shape[1]]
        out = cw[CONV_F:CONV_F + 1] + cw[0:1] * u[hr:]
        for j in range(1, CONV_F):
            out = out + cw[j:j + 1] * pltpu.roll(u, j, 0)[hr:]
        return out

    acc = None
    u_next = up(0)
    for c, (lo, n) in enumerate(FFN_SLABS):
        ug, uv = u_next
        if c + 1 < len(FFN_SLABS):
            u_next = up(c + 1)
        cg = conv(ug, lo)
        cv = conv(uv, D_FF + lo)
        act = (cg * _sigmoid(cg) * cv).astype(BF16)
        part = jnp.dot(act, wd_ref[lo:lo + n, :], preferred_element_type=F32)
        acc = part if acc is None else acc + part
    ln = ln_ref[...]
    out_ref[...] = _layer_norm(ALPHA * x + (1.0 + gt_ref[...]) * acc, ln[0:1], ln[1:2])


def _conv_ffn(x, sc, sh, gt, w_up, conv_wb, w_down, ln):
    bsz, s, d = x.shape
    tm = 512
    hr = BF16_ROWS
    mod = pl.BlockSpec((None, 1, d), lambda b, i: (b, 0, 0))
    once = pl.Buffered(1)
    return pl.pallas_call(
        _ffn_kernel,
        grid=(bsz, s // tm),
        in_specs=[pl.BlockSpec((None, tm, d), lambda b, i: (b, i, 0)),
                  pl.BlockSpec((None, hr, d), lambda b, i: (b, jnp.maximum(i * (tm // hr) - 1, 0), 0)),
                  mod, mod, mod,
                  pl.BlockSpec(w_up.shape, lambda b, i: (0, 0), pipeline_mode=once),
                  pl.BlockSpec(conv_wb.shape, lambda b, i: (0, 0), pipeline_mode=once),
                  pl.BlockSpec(w_down.shape, lambda b, i: (0, 0), pipeline_mode=once),
                  pl.BlockSpec((SUBLANES, d), lambda b, i: (0, 0))],
        out_specs=pl.BlockSpec((None, tm, d), lambda b, i: (b, i, 0)),
        out_shape=jax.ShapeDtypeStruct((bsz, s, d), F32),
        compiler_params=_cparams(("arbitrary", "arbitrary")),
        name="conv_ffn_ln",
    )(x, x, sc, sh, gt, w_up, conv_wb, w_down, ln)


def _rows(vectors, n_rows=SUBLANES):
    m = jnp.stack([v.astype(F32) for v in vectors], axis=0)
    return jnp.pad(m, ((0, n_rows - m.shape[0]), (0, 0)))


def _block_diag(w, per):
    g, n, _ = w.shape
    w = w.reshape(g // per, per, n, n)
    eye = jnp.eye(per, dtype=w.dtype)
    out = jnp.einsum('cpij,pq->cpiqj', w, eye)
    return out.reshape(g // per, per * n, per * n)


def kernel(x, c, positions, mod_w, mod_b, w_in, w_in_vres, conv_a_w, conv_a_b, lru_wa, lru_ba, lru_wx, lru_bx, lru_lambda, rwkv_mu, mu_vres, w0, w2, a0, a2, g2, v0, v2, k_k, k_a, r_k, ln_x_w, ln_x_b, proj_a, proj_b, proj_c, w_o, ln1_w, ln1_b, ffn_up, ffn_conv_w, ffn_conv_b, ffn_down, ln2_w, ln2_b):
    bsz, s, d = x.shape
    mod = _modulation(c, mod_w, mod_b)
    half = ROT_DIM // 2
    inv_freq = ROPE_THETA ** (-jnp.arange(half, dtype=F32) / half)
    lane = jnp.arange(LANES)
    freq_lane = jnp.where((lane % HEAD_DIM) < ROT_DIM, inv_freq[lane % half], 0.0).reshape(1, LANES)
    rope_tabs = _rope_tables(positions.reshape(bsz, s, 1), freq_lane)
    zero_d = jnp.zeros((d,), F32)

    v_first = None
    for l in range(DEPTH):
        m6 = mod[l].reshape(bsz, 6, 1, d)
        sh1, sc1, gt1, sh2, sc2, gt2 = (m6[:, i] for i in range(6))

        wl = w_in[l]
        w_main = jnp.concatenate([wl[:, 0:2048], wl[:, 4352:7424], wl[:, 7680:10752],
                                  wl[:, 7424:7680]], axis=1).astype(BF16)
        if l == 0:
            w_vres = None
        else:
            w_vres = jnp.pad(w_in_vres[l - 1], ((0, 0), (0, LANES - MV_LORA))).astype(BF16)
        z = _main_projection(x, sc1, sh1, w_main)
        qkv_groups, zv1 = _attn_projection(x, sc1, sh1, wl[:, 2048:4352].astype(BF16), w_vres,
                                           rope_tabs)

        per = 256 // RNN_BLOCK
        y_a = _rglru(z, conv_a_w[l],
                     _rows([conv_a_b[l], lru_ba[l], lru_bx[l], lru_lambda[l]]),
                     _block_diag(lru_wa[l], per).astype(BF16),
                     _block_diag(lru_wx[l], per).astype(BF16))

        att = [_attention_group(qkv_groups[g], g) for g in range(N_GROUPS)]

        mu = rwkv_mu[l]
        vec = _rows([mu[0:1024], mu[1024:2048], mu[2048:3072], w0[l], a0[l], k_k[l], k_a[l],
                     v0[l - 1] if l > 0 else zero_d, r_k[l].reshape(-1), ln_x_w[l], ln_x_b[l]],
                    n_rows=2 * SUBLANES)
        mu_l = mu[3072:3328].reshape(1, 2 * LANES)
        w2p = jnp.pad(w2[l], ((0, LANES - LORA_W), (0, 0))).astype(BF16)
        a2p = jnp.pad(a2[l], ((LORA_W, 0), (0, 0))).astype(BF16)
        if l == 0:
            y_c, v_first = _rwkv(z, None, None, vec, mu_l, w2p, a2p, g2[l].astype(BF16), None, None)
        else:
            mu_v1 = jnp.pad(mu_vres[l - 1], (0, LANES - MV_LORA)).reshape(1, LANES)
            v2p = jnp.pad(v2[l - 1], ((0, LANES - MV_LORA), (0, 0))).astype(BF16)
            y_c, _ = _rwkv(z, zv1, v_first, vec, mu_l, w2p, a2p, g2[l].astype(BF16), mu_v1, v2p)

        x = _merge(x, z, y_a, y_c, att, gt1, proj_a[l].astype(BF16), proj_b[l].astype(BF16),
                   proj_c[l].astype(BF16), w_o[l].astype(BF16), _rows([ln1_w[l], ln1_b[l]]))

        conv_wb = jnp.concatenate([ffn_conv_w[l], ffn_conv_b[l][None, :]], axis=0)
        conv_wb = jnp.pad(conv_wb, ((0, SUBLANES - conv_wb.shape[0]), (0, 0)))
        x = _conv_ffn(x, sc2, sh2, gt2, ffn_up[l].astype(BF16), conv_wb,
                      ffn_down[l].astype(BF16), _rows([ln2_w[l], ln2_b[l]]))
    return x
```

```python
import functools

import jax
import jax.numpy as jnp
from jax import lax
from jax.experimental import pallas as pl
from jax.experimental.pallas import tpu as pltpu

F32 = jnp.float32
BF16 = jnp.bfloat16

D_MODEL = 1024
DEPTH = 2
D_RNN = 1024
RNN_BLOCK = 64
CONV_A = 4
LRU_C = 8.0
HEAD_DIM = 64
ATTN_GROUPS = ((128, 1), (512, 4), (2048, 16))
HEADS_PER_GROUP = 4
N_GROUPS = 3
D_ATTN = 768
D_ATTN_OUT = 256
ROT_DIM = 16
ROPE_THETA = 500000.0
D_RWKV = 1024
LORA_W = 64
LORA_A = 64
LORA_G = 128
MV_LORA = 32
RWKV_GN_EPS = 64e-5
DECAY_SCALE = 0.6065306597126334
D_FF = 2816
CONV_F = 3
ALPHA = (2 * DEPTH) ** 0.25
LN_EPS = 1e-5

LANES = 128
SUBLANES = 8
BF16_ROWS = 16

OFF_XA = 0
OFF_GA = 1024
OFF_R = 2048
OFF_K = 3072
OFF_V = 4096
OFF_GATE = 5120
OFF_LORA = 8192
N_Z = 8448
MAIN_TILE = 2816

FFN_SLABS = ((0, 1536), (1536, 1280))
WKV_CHUNK = 64
RWKV_TILE = 256
ATTN_BLOCK = 128
ATTN_PER_STEP = 8
VMEM_LIMIT = 52 * 1024 * 1024


def _cparams(sem):
    return pltpu.CompilerParams(dimension_semantics=sem, vmem_limit_bytes=VMEM_LIMIT)


def _dot(a, b):
    return jnp.dot(a.astype(BF16), b.astype(BF16), preferred_element_type=F32)


def _dot_nt(a, b):
    return lax.dot_general(a.astype(BF16), b.astype(BF16), (((1,), (1,)), ((), ())),
                           preferred_element_type=F32)


def _dot_tn(a, b):
    return lax.dot_general(a.astype(BF16), b.astype(BF16), (((0,), (0,)), ((), ())),
                           preferred_element_type=F32)


def _split3(x):
    p0 = x.astype(BF16)
    r1 = x - p0.astype(F32)
    p1 = r1.astype(BF16)
    p2 = (r1 - p1.astype(F32)).astype(BF16)
    return p0, p1, p2


def _head_ones():
    i = lax.broadcasted_iota(jnp.int32, (LANES, LANES), 0) // HEAD_DIM
    j = lax.broadcasted_iota(jnp.int32, (LANES, LANES), 1) // HEAD_DIM
    return jnp.where(i == j, 1.0, 0.0).astype(BF16)


def _head_sum(x, ones):
    rows = x.shape[0]
    n = x.shape[1] // LANES
    xs = jnp.concatenate([x[:, c * LANES:(c + 1) * LANES] for c in range(n)], axis=0)
    hi = xs.astype(BF16)
    lo = (xs - hi.astype(F32)).astype(BF16)
    out = jnp.dot(jnp.concatenate([hi, lo], axis=0), ones, preferred_element_type=F32)
    tot = out[:n * rows] + out[n * rows:]
    return jnp.concatenate([tot[c * rows:(c + 1) * rows] for c in range(n)], axis=1)


def _softplus(x):
    return jnp.maximum(x, 0.0) + jnp.log1p(jnp.exp(-jnp.abs(x)))


def _sigmoid(x):
    return 0.5 * jnp.tanh(0.5 * x) + 0.5


def _layer_norm(x, w, b):
    mu = jnp.mean(x, axis=-1, keepdims=True)
    d = x - mu
    var = jnp.mean(d * d, axis=-1, keepdims=True)
    return d * lax.rsqrt(var + LN_EPS) * w + b


def _mod_kernel(c_ref, w_ref, b_ref, o_ref):
    c = c_ref[...]
    s = c * jax.nn.sigmoid(c)
    s0, s1, s2 = _split3(s)
    w0, w1, w2 = _split3(w_ref[...])
    acc = jnp.dot(s0, w0, preferred_element_type=F32)
    acc += jnp.dot(s0, w1, preferred_element_type=F32) + jnp.dot(s1, w0, preferred_element_type=F32)
    acc += (jnp.dot(s1, w1, preferred_element_type=F32) + jnp.dot(s0, w2, preferred_element_type=F32)
            + jnp.dot(s2, w0, preferred_element_type=F32))
    o_ref[...] = acc + b_ref[...]


def _modulation(c, mod_w, mod_b):
    depth, d, n = mod_w.shape
    bsz = c.shape[0]
    rows = BF16_ROWS
    tn = 1536
    c_pad = jnp.pad(c, ((0, rows - bsz), (0, 0)))
    out = pl.pallas_call(
        _mod_kernel,
        grid=(depth, n // tn),
        in_specs=[pl.BlockSpec((rows, d), lambda l, j: (0, 0)),
                  pl.BlockSpec((None, d, tn), lambda l, j: (l, 0, j)),
                  pl.BlockSpec((None, 1, tn), lambda l, j: (l, 0, j))],
        out_specs=pl.BlockSpec((None, rows, tn), lambda l, j: (l, 0, j)),
        out_shape=jax.ShapeDtypeStruct((depth, rows, n), F32),
        compiler_params=_cparams(("arbitrary", "arbitrary")),
        name="adaln_mod",
    )(c_pad, mod_w, mod_b.reshape(depth, 1, n))
    return out[:, :bsz]


def _main_proj_kernel(x_ref, sc_ref, sh_ref, w_ref, z_ref):
    h = (x_ref[...] * (1.0 + sc_ref[...]) + sh_ref[...]).astype(BF16)
    z_ref[...] = jnp.dot(h, w_ref[...], preferred_element_type=F32)


def _main_projection(x, sc, sh, w):
    bsz, s, d = x.shape
    n = w.shape[1]
    tm, tn = 1024, MAIN_TILE
    return pl.pallas_call(
        _main_proj_kernel,
        grid=(n // tn, bsz, s // tm),
        in_specs=[pl.BlockSpec((None, tm, d), lambda j, b, i: (b, i, 0)),
                  pl.BlockSpec((None, 1, d), lambda j, b, i: (b, 0, 0)),
                  pl.BlockSpec((None, 1, d), lambda j, b, i: (b, 0, 0)),
                  pl.BlockSpec((d, tn), lambda j, b, i: (0, j))],
        out_specs=pl.BlockSpec((None, tm, tn), lambda j, b, i: (b, i, j)),
        out_shape=jax.ShapeDtypeStruct((bsz, s, n), F32),
        compiler_params=_cparams(("arbitrary", "arbitrary", "arbitrary")),
        name="main_projection",
    )(x, sc, sh, w)


def _attn_proj_kernel(x_ref, sc_ref, sh_ref, w_ref, cos_ref, sa_ref, sb_ref, *rest, has_vres):
    if has_vres:
        wv_ref, g0_ref, g1_ref, g2_ref, zv_ref, r_scr = rest
    else:
        g0_ref, g1_ref, g2_ref, r_scr = rest
    tm = x_ref.shape[0]
    width = HEADS_PER_GROUP * HEAD_DIM
    half = ROT_DIM // 2
    h = (x_ref[...] * (1.0 + sc_ref[...]) + sh_ref[...]).astype(BF16)
    if has_vres:
        zv_ref[...] = jnp.dot(h, wv_ref[...], preferred_element_type=F32)
    acc = jnp.dot(h, w_ref[...], preferred_element_type=F32)
    n_slab = D_ATTN // LANES
    for sec in range(3):
        scale = HEAD_DIM ** -0.5 if sec == 0 else 1.0
        for c in range(n_slab):
            lo = sec * D_ATTN + c * LANES
            t = acc[:, lo:lo + LANES]
            if sec < 2:
                t = (t * cos_ref[...] + pltpu.roll(t, half, 1) * sa_ref[...]
                     + pltpu.roll(t, LANES - half, 1) * sb_ref[...]) * scale
            r_scr[sec * n_slab + c] = t
    per = width // LANES
    for sec in range(3):
        for g, g_ref in enumerate((g0_ref, g1_ref, g2_ref)):
            dil = g_ref.shape[0]
            for r in range(dil):
                rows = pl.ds(r, tm // dil, stride=dil) if dil > 1 else slice(None)
                for c in range(per):
                    lo = sec * width + c * LANES
                    g_ref[r, :, lo:lo + LANES] = r_scr[sec * n_slab + g * per + c, rows, :].astype(BF16)


def _attn_projection(x, sc, sh, w, w_vres, rope_tabs):
    bsz, s, d = x.shape
    n = w.shape[1]
    tm = 512
    has_vres = w_vres is not None
    tab = pl.BlockSpec((None, tm, LANES), lambda b, i: (b, i, 0))
    in_specs = [pl.BlockSpec((None, tm, d), lambda b, i: (b, i, 0)),
                pl.BlockSpec((None, 1, d), lambda b, i: (b, 0, 0)),
                pl.BlockSpec((None, 1, d), lambda b, i: (b, 0, 0)),
                pl.BlockSpec((d, n), lambda b, i: (0, 0)),
                tab, tab, tab]
    out_specs, out_shape = [], []
    for _, dil in ATTN_GROUPS:
        out_specs.append(pl.BlockSpec((None, dil, tm // dil, D_ATTN), lambda b, i: (b, 0, i, 0)))
        out_shape.append(jax.ShapeDtypeStruct((bsz, dil, s // dil, D_ATTN), BF16))
    args = [x, sc, sh, w, *rope_tabs]
    if has_vres:
        in_specs.append(pl.BlockSpec((d, LANES), lambda b, i: (0, 0)))
        out_specs.append(pl.BlockSpec((None, tm, LANES), lambda b, i: (b, i, 0)))
        out_shape.append(jax.ShapeDtypeStruct((bsz, s, LANES), F32))
        args.append(w_vres)
    outs = pl.pallas_call(
        functools.partial(_attn_proj_kernel, has_vres=has_vres),
        grid=(bsz, s // tm),
        in_specs=in_specs,
        out_specs=out_specs,
        out_shape=out_shape,
        scratch_shapes=[pltpu.VMEM((n // LANES, tm, LANES), F32)],
        compiler_params=_cparams(("arbitrary", "arbitrary")),
        name="attn_projection",
    )(*args)
    return outs[0:3], (outs[3] if has_vres else None)


def _rglru_kernel(xa_ref, ga_ref, cw_ref, vec_ref, wa_ref, wx_ref, y_ref, xbuf, hc):
    t = pl.program_id(2)
    tt, cwid = xa_ref.shape

    @pl.when(t == 0)
    def _():
        xbuf[0:SUBLANES, :] = jnp.zeros((SUBLANES, cwid), F32)
        hc[...] = jnp.zeros_like(hc)

    xa = xa_ref[...]
    xbuf[SUBLANES:SUBLANES + tt, :] = xa
    cw = cw_ref[...]
    vec = vec_ref[...]
    xc = vec[0:1] + cw[0:1] * xa
    for j in range(1, CONV_A):
        xc = xc + cw[j:j + 1] * xbuf[SUBLANES - j:SUBLANES - j + tt, :]
    xbuf[0:SUBLANES, :] = xbuf[tt:tt + SUBLANES, :]

    xb = xc.astype(BF16)
    r = _sigmoid(jnp.dot(xb, wa_ref[...], preferred_element_type=F32) + vec[1:2])
    i = _sigmoid(jnp.dot(xb, wx_ref[...], preferred_element_type=F32) + vec[2:3])
    log_a = (-LRU_C) * r * _softplus(-vec[3:4])
    a = jnp.exp(log_a)
    om = -jnp.tanh(log_a) * (a * a + 1.0)
    b = jnp.where(om > 0.0, om * lax.rsqrt(om), 0.0) * (i * xc)

    ng = tt // SUBLANES
    a = a.reshape(ng, SUBLANES, cwid)
    b = b.reshape(ng, SUBLANES, cwid)
    rows = lax.broadcasted_iota(jnp.int32, (ng, SUBLANES, cwid), 1)
    step = 1
    while step < SUBLANES:
        a_sh = pltpu.roll(a, step, 1)
        b_sh = pltpu.roll(b, step, 1)
        m = rows >= step
        b = jnp.where(m, a * b_sh + b, b)
        a = jnp.where(m, a * a_sh, a)
        step *= 2
    g = ga_ref[...]
    gelu = 0.5 * g * (1.0 + jnp.tanh(0.7978845608028654 * (g + 0.044715 * (g * g * g))))
    carry = hc[0:1, :]
    hs = []
    for i in range(ng):
        h = a[i] * carry + b[i]
        carry = h[SUBLANES - 1:SUBLANES, :]
        hs.append(h)
    hc[0:1, :] = carry
    y_ref[...] = (jnp.concatenate(hs, axis=0) * gelu).astype(BF16)


def _rglru(z, conv_w, vec, wa_bd, wx_bd):
    bsz, s, _ = z.shape
    tt, cwid = 1024, 256
    nc = D_RNN // cwid
    return pl.pallas_call(
        _rglru_kernel,
        grid=(bsz, nc, s // tt),
        in_specs=[pl.BlockSpec((None, tt, cwid), lambda b, c, t: (b, t, OFF_XA // cwid + c)),
                  pl.BlockSpec((None, tt, cwid), lambda b, c, t: (b, t, OFF_GA // cwid + c)),
                  pl.BlockSpec((CONV_A, cwid), lambda b, c, t: (0, c)),
                  pl.BlockSpec((SUBLANES, cwid), lambda b, c, t: (0, c)),
                  pl.BlockSpec((None, cwid, cwid), lambda b, c, t: (c, 0, 0)),
                  pl.BlockSpec((None, cwid, cwid), lambda b, c, t: (c, 0, 0))],
        out_specs=pl.BlockSpec((None, tt, cwid), lambda b, c, t: (b, t, c)),
        out_shape=jax.ShapeDtypeStruct((bsz, s, D_RNN), BF16),
        scratch_shapes=[pltpu.VMEM((tt + SUBLANES, cwid), F32), pltpu.VMEM((SUBLANES, cwid), F32)],
        compiler_params=_cparams(("arbitrary", "arbitrary", "arbitrary")),
        name="rglru",
    )(z, z, conv_w, vec, wa_bd, wx_bd)


def _rope_table_kernel(pos_ref, freq_ref, cos_ref, sa_ref, sb_ref):
    tt = pos_ref.shape[0]
    ang = pos_ref[...].astype(F32) * freq_ref[...]
    lane = lax.broadcasted_iota(jnp.int32, (tt, LANES), 1) % HEAD_DIM
    sn = jnp.sin(ang)
    half = ROT_DIM // 2
    cos_ref[...] = jnp.cos(ang)
    sa_ref[...] = jnp.where((lane >= half) & (lane < ROT_DIM), sn, 0.0)
    sb_ref[...] = jnp.where(lane < half, -sn, 0.0)


def _rope_tables(pos3, freq_lane):
    bsz, s, _ = pos3.shape
    tt = 1024
    out = pl.BlockSpec((None, tt, LANES), lambda b, t: (b, t, 0))
    return pl.pallas_call(
        _rope_table_kernel,
        grid=(bsz, s // tt),
        in_specs=[pl.BlockSpec((None, tt, 1), lambda b, t: (b, t, 0)),
                  pl.BlockSpec((1, LANES), lambda b, t: (0, 0))],
        out_specs=[out] * 3,
        out_shape=[jax.ShapeDtypeStruct((bsz, s, LANES), F32)] * 3,
        compiler_params=_cparams(("arbitrary", "arbitrary")),
        name="rope_tables",
    )(pos3, freq_lane)


def _attn_kernel(q_ref, kp_ref, kc_ref, vp_ref, vc_ref, o_ref, lse_ref):
    nq = ATTN_BLOCK
    n_res, n_rows, width = q_ref.shape
    nh = HEADS_PER_GROUP
    nk = 2 * nq
    head = lax.broadcasted_iota(jnp.int32, (nq, width), 1) // HEAD_DIM
    qi = lax.broadcasted_iota(jnp.int32, (nh * nq, nk), 0) % nq
    kj = lax.broadcasted_iota(jnp.int32, (nh * nq, nk), 1)
    band = (kj >= qi) & (kj <= qi + nq)
    for rr in range(n_res):
        k_all = jnp.concatenate([kp_ref[rr], kc_ref[rr]], axis=0)
        v_all = jnp.concatenate([vp_ref[rr], vc_ref[rr]], axis=0)
        for j in range(n_rows // nq):
            q = q_ref[rr, j * nq:(j + 1) * nq, :]
            k = k_all[j * nq:j * nq + nk]
            v = v_all[j * nq:j * nq + nk]
            qcat = jnp.concatenate([jnp.where(head == h, q, jnp.zeros_like(q))
                                    for h in range(nh)], axis=0)
            s = lax.dot_general(qcat, k, (((1,), (1,)), ((), ())), preferred_element_type=F32)
            if j == 0:
                valid = band & ((pl.program_id(2) > 0) | (kj >= nq))
            else:
                valid = band
            s = jnp.where(valid, s, -1e30)
            m = jnp.max(s, axis=1, keepdims=True)
            p = jnp.exp(s - m)
            l = jnp.sum(p, axis=1, keepdims=True)
            pv = jnp.dot(p.astype(BF16), v, preferred_element_type=F32) / l
            lse_rows = m + jnp.log(l)
            o = jnp.zeros((nq, width), F32)
            lse = jnp.zeros((nq, width), F32)
            for h in range(nh):
                hm = head == h
                rows = slice(h * nq, (h + 1) * nq)
                o = jnp.where(hm, pv[rows], o)
                lse = jnp.where(hm, lse_rows[rows], lse)
            o_ref[rr, j * nq:(j + 1) * nq, :] = o
            lse_ref[rr, j * nq:(j + 1) * nq, :] = lse


def _attention_group(qkv, g):
    bsz, dil, sub, _ = qkv.shape
    width = HEADS_PER_GROUP * HEAD_DIM
    nb = ATTN_BLOCK

    per = min(ATTN_PER_STEP, sub // nb)
    n_res = min(dil, ATTN_PER_STEP // per)

    def cur(sec):
        return lambda b, r, m: (b, r, m, sec)

    def prev(sec):
        return lambda b, r, m: (b, r, jnp.maximum(m * per - 1, 0), sec)

    blk = (None, n_res, per * nb, width)
    pblk = (None, n_res, nb, width)
    return pl.pallas_call(
        _attn_kernel,
        grid=(bsz, dil // n_res, sub // (per * nb)),
        in_specs=[pl.BlockSpec(blk, cur(0)),
                  pl.BlockSpec(pblk, prev(1)), pl.BlockSpec(blk, cur(1)),
                  pl.BlockSpec(pblk, prev(2)), pl.BlockSpec(blk, cur(2))],
        out_specs=[pl.BlockSpec(blk, cur(0)), pl.BlockSpec(blk, cur(0))],
        out_shape=[jax.ShapeDtypeStruct((bsz, dil, sub, width), F32)] * 2,
        compiler_params=_cparams(("arbitrary", "arbitrary", "arbitrary")),
        name=f"dilated_attention_g{g}",
    )(qkv, qkv, qkv, qkv, qkv)


def _token_shift(z, halo, first, mu):
    ext = jnp.concatenate([jnp.where(first, 0.0, halo), z], axis=0)
    zp = pltpu.roll(ext, 1, 0)[SUBLANES:]
    return z + (zp - z) * mu


def _wkv_chunk(r, k, v, a, b, lw, s_scr):
    L = r.shape[0]
    n_pairs = r.shape[1] // LANES
    ti = lax.broadcasted_iota(jnp.int32, (L, L), 0)
    tj = lax.broadcasted_iota(jnp.int32, (L, L), 1)
    tri = jnp.where(ti >= tj, 1.0, 0.0).astype(BF16)
    l0, l1, l2 = _split3(lw)
    lc = (jnp.dot(tri, l0, preferred_element_type=F32) + jnp.dot(tri, l1, preferred_element_type=F32)
          + jnp.dot(tri, l2, preferred_element_type=F32))
    lc_last = lc[L - 1:L, :]
    e_inv = jnp.exp(-lc)
    e_tail = jnp.exp(lc_last - lc)
    a_hat = a * jnp.exp(lc - lw)
    r_hat = r * jnp.exp(lc)
    b_hat = b * e_inv
    k_hat = k * e_inv
    b_til = b * e_tail
    k_til = k * e_tail
    d_last = jnp.exp(lc_last)

    even = lax.broadcasted_iota(jnp.int32, (L, LANES), 1) < HEAD_DIM

    def cat(x):
        x = x.astype(BF16)
        zero = jnp.zeros_like(x)
        return jnp.concatenate([jnp.where(even, x, zero), jnp.where(even, zero, x)], axis=0)

    def stack(*xs):
        return jnp.concatenate([x.astype(BF16) for x in xs], axis=0)

    ti = lax.broadcasted_iota(jnp.int32, (L, 2 * L), 0)
    tj = lax.broadcasted_iota(jnp.int32, (L, 2 * L), 1) % L
    strict = ti > tj
    incl = ti >= tj
    eye = jnp.where(ti == tj, 1.0, 0.0)
    pi = lax.broadcasted_iota(jnp.int32, (LANES, LANES), 0)
    qi = lax.broadcasted_iota(jnp.int32, (LANES, LANES), 1)
    same_head = (pi // HEAD_DIM) == (qi // HEAD_DIM)

    pairs = range(n_pairs)
    sls = [slice(p * LANES, (p + 1) * LANES) for p in pairs]
    v_c = [cat(v[:, sl]) for sl in sls]
    gram = [_dot_nt(stack(a_hat[:, sl], r_hat[:, sl]),
                    jnp.concatenate([cat(b_hat[:, sl]), cat(k_hat[:, sl])], axis=0))
            for sl in sls]
    nmat = [jnp.where(strict, gram[p][:L, :LANES], 0.0) for p in pairs]
    a_ak = [jnp.where(strict, gram[p][:L, LANES:], 0.0) for p in pairs]
    a_rb = [jnp.where(incl, gram[p][L:, :LANES], 0.0) for p in pairs]
    a_rk = [jnp.where(incl, gram[p][L:, LANES:], 0.0) for p in pairs]
    av = [_dot(stack(a_ak[p], a_rk[p]), v_c[p]) for p in pairs]
    tinv = [eye + nmat[p] for p in pairs]
    npow = [_dot(nmat[p], cat(nmat[p])) for p in pairs]
    span = 4
    while span < L:
        prod = [_dot(stack(tinv[p], npow[p]), cat(npow[p])) for p in pairs]
        tinv = [tinv[p] + prod[p][:L] for p in pairs]
        npow = [prod[p][L:] for p in pairs]
        span *= 2
    tinv = [tinv[p] + _dot(tinv[p], cat(npow[p])) for p in pairs]
    wu = [_dot(tinv[p], jnp.concatenate([cat(a_hat[:, sls[p]]), cat(av[p][:L])], axis=1))
          for p in pairs]
    qy = [_dot(a_rb[p], jnp.concatenate([cat(wu[p][:, :LANES]), cat(wu[p][:, LANES:])], axis=1))
          for p in pairs]
    mc = [_dot_tn(stack(b_til[:, sls[p]], k_til[:, sls[p]]),
                  jnp.concatenate([wu[p].astype(BF16),
                                   jnp.concatenate([jnp.zeros((L, LANES), BF16),
                                                    v[:, sls[p]].astype(BF16)], axis=1)], axis=0))
          for p in pairs]
    ys = []
    for p in pairs:
        dl = jnp.broadcast_to(d_last[:, sls[p]], (LANES, LANES))
        m_mat = jnp.where(same_head, mc[p][:, :LANES], 0.0) + jnp.where(pi == qi, dl, 0.0)
        c_mat = jnp.where(same_head, mc[p][:, LANES:], 0.0)
        q_t = r_hat[:, sls[p]] + qy[p][:, :LANES]
        out = _dot(stack(q_t, m_mat), s_scr[p])
        ys.append(out[:L] + qy[p][:, LANES:] + av[p][L:])
        s_scr[p] = out[L:] + c_mat

    return jnp.concatenate(ys, axis=1)


def _rwkv_kernel(*refs, has_vres):
    if has_vres:
        (zr_ref, zk_ref, zv_ref, zl_ref, hr_ref, hk_ref, hv_ref, hl_ref, vec_ref, mul_ref,
         w2_ref, a2_ref, g2_ref, zv1_ref, hv1_ref, vf_ref, muv_ref, v2_ref, y_ref, s_scr) = refs
    else:
        (zr_ref, zk_ref, zv_ref, zl_ref, hr_ref, hk_ref, hv_ref, hl_ref, vec_ref, mul_ref,
         w2_ref, a2_ref, g2_ref, y_ref, vf_out, s_scr) = refs
    first = pl.program_id(1) == 0

    @pl.when(first)
    def _():
        s_scr[...] = jnp.zeros_like(s_scr)

    vec = vec_ref[...]
    r = _token_shift(zr_ref[...], hr_ref[...], first, vec[0:1])
    k = _token_shift(zk_ref[...], hk_ref[...], first, vec[1:2])
    v = _token_shift(zv_ref[...], hv_ref[...], first, vec[2:3])
    lora = _token_shift(zl_ref[...], hl_ref[...], first, mul_ref[...])
    wa_in = lora[:, 0:LANES]
    w_pre = vec[3:4] + _dot(jnp.tanh(wa_in), w2_ref[...])
    lw = (-DECAY_SCALE) * _sigmoid(w_pre)
    alpha = _sigmoid(vec[4:5] + _dot(wa_in, a2_ref[...]))
    g = _dot(_sigmoid(lora[:, LANES:2 * LANES]), g2_ref[...])
    if has_vres:
        zv1 = _token_shift(zv1_ref[...], hv1_ref[...], first, muv_ref[...])
        mix = _sigmoid(vec[7:8] + _dot(zv1, v2_ref[...]))
        v = v + (vf_ref[...] - v) * mix
    else:
        vf_out[...] = v
    ones = _head_ones()
    kk = k * vec[5:6]
    kk = kk * lax.rsqrt(_head_sum(kk * kk, ones) + 1e-12)
    k = k * (1.0 + (alpha - 1.0) * vec[6:7])
    a = -kk
    b = kk * alpha

    L = WKV_CHUNK
    ys = []
    for c in range(r.shape[0] // L):
        rows = slice(c * L, (c + 1) * L)
        ys.append(_wkv_chunk(r[rows], k[rows], v[rows], a[rows], b[rows], lw[rows], s_scr))
    y = jnp.concatenate(ys, axis=0)

    inv_n = 1.0 / HEAD_DIM
    mu = _head_sum(y, ones) * inv_n
    dy = y - mu
    var = _head_sum(dy * dy, ones) * inv_n
    gn = dy * lax.rsqrt(var + RWKV_GN_EPS) * vec[9:10] + vec[10:11]
    bonus = _head_sum(r * k * vec[8:9], ones) * v
    y_ref[...] = ((gn + bonus) * g).astype(BF16)


def _rwkv(z, zv1, v_first, vec, mu_l, w2p, a2p, g2, mu_v1, v2p):
    bsz, s, _ = z.shape
    tt = RWKV_TILE
    d = D_RWKV
    has_vres = zv1 is not None
    hstep = tt // SUBLANES

    def col(off, w):
        return lambda b, t: (b, t, off // w)

    def hcol(off, w):
        return lambda b, t: (b, jnp.maximum(t * hstep - 1, 0), off // w)

    lw = 2 * LANES
    in_specs = [pl.BlockSpec((None, tt, d), col(OFF_R, d)),
                pl.BlockSpec((None, tt, d), col(OFF_K, d)),
                pl.BlockSpec((None, tt, d), col(OFF_V, d)),
                pl.BlockSpec((None, tt, lw), col(OFF_LORA, lw)),
                pl.BlockSpec((None, SUBLANES, d), hcol(OFF_R, d)),
                pl.BlockSpec((None, SUBLANES, d), hcol(OFF_K, d)),
                pl.BlockSpec((None, SUBLANES, d), hcol(OFF_V, d)),
                pl.BlockSpec((None, SUBLANES, lw), hcol(OFF_LORA, lw)),
                pl.BlockSpec(vec.shape, lambda b, t: (0, 0)),
                pl.BlockSpec((1, lw), lambda b, t: (0, 0)),
                pl.BlockSpec((LANES, d), lambda b, t: (0, 0)),
                pl.BlockSpec((LANES, d), lambda b, t: (0, 0)),
                pl.BlockSpec((LANES, d), lambda b, t: (0, 0))]
    args = [z, z, z, z, z, z, z, z, vec, mu_l, w2p, a2p, g2]
    row = pl.BlockSpec((None, tt, d), lambda b, t: (b, t, 0))
    out_specs = [row]
    out_shape = [jax.ShapeDtypeStruct((bsz, s, d), BF16)]
    if has_vres:
        in_specs += [pl.BlockSpec((None, tt, LANES), lambda b, t: (b, t, 0)),
                     pl.BlockSpec((None, SUBLANES, LANES),
                                  lambda b, t: (b, jnp.maximum(t * hstep - 1, 0), 0)),
                     row,
                     pl.BlockSpec((1, LANES), lambda b, t: (0, 0)),
                     pl.BlockSpec((LANES, d), lambda b, t: (0, 0))]
        args += [zv1, zv1, v_first, mu_v1, v2p]
    else:
        out_specs.append(row)
        out_shape.append(jax.ShapeDtypeStruct((bsz, s, d), F32))
    outs = pl.pallas_call(
        functools.partial(_rwkv_kernel, has_vres=has_vres),
        grid=(bsz, s // tt),
        in_specs=in_specs,
        out_specs=out_specs,
        out_shape=out_shape,
        scratch_shapes=[pltpu.VMEM((d // LANES, LANES, LANES), F32)],
        compiler_params=_cparams(("arbitrary", "arbitrary")),
        name="rwkv7",
    )(*args)
    return outs[0], (v_first if has_vres else outs[1])


def _interleave(ref, scr):
    dil, n, w = ref.shape
    if dil == 1:
        return ref[0]
    for r in range(dil):
        for c in range(w // LANES):
            scr[c, pl.ds(r, n, stride=dil), :] = ref[r, :, c * LANES:(c + 1) * LANES]
    return jnp.concatenate([scr[c] for c in range(w // LANES)], axis=1)


def _merge_kernel(x_ref, ya_ref, yc_ref, o0_ref, o1_ref, o2_ref, l0_ref, l1_ref, l2_ref,
                  ga_ref, gb_ref, gc_ref, gt_ref, pa_ref, pb_ref, pc_ref, wo_ref, ln_ref, out_ref,
                  so1, so2, sl1, sl2):
    l0 = l0_ref[0]
    l1 = _interleave(l1_ref, sl1)
    l2 = _interleave(l2_ref, sl2)
    o0 = o0_ref[0]
    o1 = _interleave(o1_ref, so1)
    o2 = _interleave(o2_ref, so2)
    lm = jnp.maximum(jnp.maximum(l0, l1), l2)
    e0 = jnp.exp(l0 - lm)
    e1 = jnp.exp(l1 - lm)
    e2 = jnp.exp(l2 - lm)
    yb = ((e0 * o0 + e1 * o1 + e2 * o2) / (e0 + e1 + e2)).astype(BF16)
    ln = ln_ref[...]
    gt = 1.0 + gt_ref[...]
    n_half = 2
    hrows = x_ref.shape[0] // n_half
    for hf in range(n_half):
        rows = slice(hf * hrows, (hf + 1) * hrows)
        merged = (jax.nn.sigmoid(ga_ref[rows, :])
                  * jnp.dot(ya_ref[rows, :], pa_ref[...], preferred_element_type=F32)
                  + jax.nn.sigmoid(gb_ref[rows, :])
                  * jnp.dot(yb[rows], pb_ref[...], preferred_element_type=F32)
                  + jax.nn.sigmoid(gc_ref[rows, :])
                  * jnp.dot(yc_ref[rows, :], pc_ref[...], preferred_element_type=F32))
        y = _dot(merged, wo_ref[...])
        out_ref[rows, :] = _layer_norm(ALPHA * x_ref[rows, :] + gt * y, ln[0:1], ln[1:2])


def _merge(x, z, ya, yc, att, gt, pa, pb, pc, wo, ln):
    bsz, s, d = x.shape
    tm = 512
    wa = D_ATTN_OUT
    row = lambda w, off=0: pl.BlockSpec((None, tm, w), lambda b, i: (b, i, off // w))
    const = lambda shp: pl.BlockSpec(shp, lambda b, i: (0,) * len(shp))
    res = lambda dil: pl.BlockSpec((None, dil, tm // dil, wa), lambda b, i: (b, 0, i, 0))
    (o0, l0), (o1, l1), (o2, l2) = att
    att_specs = [res(dil) for _, dil in ATTN_GROUPS] * 2
    return pl.pallas_call(
        _merge_kernel,
        grid=(bsz, s // tm),
        scratch_shapes=[pltpu.VMEM((wa // LANES, tm, LANES), F32)] * 4,
        in_specs=[row(d), row(d), row(d)] + att_specs
                 + [row(d, OFF_GATE), row(d, OFF_GATE + d), row(d, OFF_GATE + 2 * d),
                    pl.BlockSpec((None, 1, d), lambda b, i: (b, 0, 0)),
                    const((d, d)), const((wa, d)), const((d, d)), const((d, d)),
                    const((SUBLANES, d))],
        out_specs=row(d),
        out_shape=jax.ShapeDtypeStruct((bsz, s, d), F32),
        compiler_params=_cparams(("arbitrary", "arbitrary")),
        name="merge_proj_ln",
    )(x, ya, yc, o0, o1, o2, l0, l1, l2, z, z, z, gt, pa, pb, pc, wo, ln)


def _ffn_kernel(x_ref, halo_ref, sc_ref, sh_ref, gt_ref, wu_ref, cw_ref, wd_ref, ln_ref, out_ref):
    i = pl.program_id(1)
    hr = halo_ref.shape[0]
    sc = 1.0 + sc_ref[...]
    sh = sh_ref[...]
    x = x_ref[...]
    hh = jnp.where(i == 0, 0.0, halo_ref[...] * sc + sh)
    h = jnp.concatenate([hh.astype(BF16), (x * sc + sh).astype(BF16)], axis=0)

    def up(c):
        lo, n = FFN_SLABS[c]
        return [jnp.dot(h, wu_ref[:, off:off + n], preferred_element_type=F32)
                for off in (lo, D_FF + lo)]

    def conv(u, off):
        cw = cw_ref[:, off:off + u.shape[1]]
        out = cw[CONV_F:CONV_F + 1] + cw[0:1] * u[hr:]
        for j in range(1, CONV_F):
            out = out + cw[j:j + 1] * pltpu.roll(u, j, 0)[hr:]
        return out

    acc = None
    u_next = up(0)
    for c, (lo, n) in enumerate(FFN_SLABS):
        ug, uv = u_next
        if c + 1 < len(FFN_SLABS):
            u_next = up(c + 1)
        cg = conv(ug, lo)
        cv = conv(uv, D_FF + lo)
        act = (cg * _sigmoid(cg) * cv).astype(BF16)
        part = jnp.dot(act, wd_ref[lo:lo + n, :], preferred_element_type=F32)
        acc = part if acc is None else acc + part
    ln = ln_ref[...]
    out_ref[...] = _layer_norm(ALPHA * x + (1.0 + gt_ref[...]) * acc, ln[0:1], ln[1:2])


def _conv_ffn(x, sc, sh, gt, w_up, conv_wb, w_down, ln):
    bsz, s, d = x.shape
    tm = 512
    hr = BF16_ROWS
    mod = pl.BlockSpec((None, 1, d), lambda b, i: (b, 0, 0))
    once = pl.Buffered(1)
    return pl.pallas_call(
        _ffn_kernel,
        grid=(bsz, s // tm),
        in_specs=[pl.BlockSpec((None, tm, d), lambda b, i: (b, i, 0)),
                  pl.BlockSpec((None, hr, d), lambda b, i: (b, jnp.maximum(i * (tm // hr) - 1, 0), 0)),
                  mod, mod, mod,
                  pl.BlockSpec(w_up.shape, lambda b, i: (0, 0), pipeline_mode=once),
                  pl.BlockSpec(conv_wb.shape, lambda b, i: (0, 0), pipeline_mode=once),
                  pl.BlockSpec(w_down.shape, lambda b, i: (0, 0), pipeline_mode=once),
                  pl.BlockSpec((SUBLANES, d), lambda b, i: (0, 0))],
        out_specs=pl.BlockSpec((None, tm, d), lambda b, i: (b, i, 0)),
        out_shape=jax.ShapeDtypeStruct((bsz, s, d), F32),
        compiler_params=_cparams(("arbitrary", "arbitrary")),
        name="conv_ffn_ln",
    )(x, x, sc, sh, gt, w_up, conv_wb, w_down, ln)


def _rows(vectors, n_rows=SUBLANES):
    m = jnp.stack([v.astype(F32) for v in vectors], axis=0)
    return jnp.pad(m, ((0, n_rows - m.shape[0]), (0, 0)))


def _block_diag(w, per):
    g, n, _ = w.shape
    w = w.reshape(g // per, per, n, n)
    eye = jnp.eye(per, dtype=w.dtype)
    out = jnp.einsum('cpij,pq->cpiqj', w, eye)
    return out.reshape(g // per, per * n, per * n)


def kernel(x, c, positions, mod_w, mod_b, w_in, w_in_vres, conv_a_w, conv_a_b, lru_wa, lru_ba, lru_wx, lru_bx, lru_lambda, rwkv_mu, mu_vres, w0, w2, a0, a2, g2, v0, v2, k_k, k_a, r_k, ln_x_w, ln_x_b, proj_a, proj_b, proj_c, w_o, ln1_w, ln1_b, ffn_up, ffn_conv_w, ffn_conv_b, ffn_down, ln2_w, ln2_b):
    bsz, s, d = x.shape
    mod = _modulation(c, mod_w, mod_b)
    half = ROT_DIM // 2
    inv_freq = ROPE_THETA ** (-jnp.arange(half, dtype=F32) / half)
    lane = jnp.arange(LANES)
    freq_lane = jnp.where((lane % HEAD_DIM) < ROT_DIM, inv_freq[lane % half], 0.0).reshape(1, LANES)
    rope_tabs = _rope_tables(positions.reshape(bsz, s, 1), freq_lane)
    zero_d = jnp.zeros((d,), F32)

    v_first = None
    for l in range(DEPTH):
        m6 = mod[l].reshape(bsz, 6, 1, d)
        sh1, sc1, gt1, sh2, sc2, gt2 = (m6[:, i] for i in range(6))

        wl = w_in[l]
        w_main = jnp.concatenate([wl[:, 0:2048], wl[:, 4352:7424], wl[:, 7680:10752],
                                  wl[:, 7424:7680]], axis=1).astype(BF16)
        if l == 0:
            w_vres = None
        else:
            w_vres = jnp.pad(w_in_vres[l - 1], ((0, 0), (0, LANES - MV_LORA))).astype(BF16)
        z = _main_projection(x, sc1, sh1, w_main)
        qkv_groups, zv1 = _attn_projection(x, sc1, sh1, wl[:, 2048:4352].astype(BF16), w_vres,
                                           rope_tabs)

        per = 256 // RNN_BLOCK
        y_a = _rglru(z, conv_a_w[l],
                     _rows([conv_a_b[l], lru_ba[l], lru_bx[l], lru_lambda[l]]),
                     _block_diag(lru_wa[l], per).astype(BF16),
                     _block_diag(lru_wx[l], per).astype(BF16))

        att = [_attention_group(qkv_groups[g], g) for g in range(N_GROUPS)]

        mu = rwkv_mu[l]
        vec = _rows([mu[0:1024], mu[1024:2048], mu[2048:3072], w0[l], a0[l], k_k[l], k_a[l],
                     v0[l - 1] if l > 0 else zero_d, r_k[l].reshape(-1), ln_x_w[l], ln_x_b[l]],
                    n_rows=2 * SUBLANES)
        mu_l = mu[3072:3328].reshape(1, 2 * LANES)
        w2p = jnp.pad(w2[l], ((0, LANES - LORA_W), (0, 0))).astype(BF16)
        a2p = jnp.pad(a2[l], ((LORA_W, 0), (0, 0))).astype(BF16)
        if l == 0:
            y_c, v_first = _rwkv(z, None, None, vec, mu_l, w2p, a2p, g2[l].astype(BF16), None, None)
        else:
            mu_v1 = jnp.pad(mu_vres[l - 1], (0, LANES - MV_LORA)).reshape(1, LANES)
            v2p = jnp.pad(v2[l - 1], ((0, LANES - MV_LORA), (0, 0))).astype(BF16)
            y_c, _ = _rwkv(z, zv1, v_first, vec, mu_l, w2p, a2p, g2[l].astype(BF16), mu_v1, v2p)

        x = _merge(x, z, y_a, y_c, att, gt1, proj_a[l].astype(BF16), proj_b[l].astype(BF16),
                   proj_c[l].astype(BF16), w_o[l].astype(BF16), _rows([ln1_w[l], ln1_b[l]]))

        conv_wb = jnp.concatenate([ffn_conv_w[l], ffn_conv_b[l][None, :]], axis=0)
        conv_wb = jnp.pad(conv_wb, ((0, SUBLANES - conv_wb.shape[0]), (0, 0)))
        x = _conv_ffn(x, sc2, sh2, gt2, ffn_up[l].astype(BF16), conv_wb,
                      ffn_down[l].astype(BF16), _rows([ln2_w[l], ln2_b[l]]))
    return x
```

```python
import functools

import jax
import jax.numpy as jnp
from jax import lax
from jax.experimental import pallas as pl
from jax.experimental.pallas import tpu as pltpu

F32 = jnp.float32
BF16 = jnp.bfloat16

D_MODEL = 1024
DEPTH = 2
D_RNN = 1024
RNN_BLOCK = 64
CONV_A = 4
LRU_C = 8.0
HEAD_DIM = 64
ATTN_GROUPS = ((128, 1), (512, 4), (2048, 16))
HEADS_PER_GROUP = 4
N_GROUPS = 3
D_ATTN = 768
D_ATTN_OUT = 256
ROT_DIM = 16
ROPE_THETA = 500000.0
D_RWKV = 1024
LORA_W = 64
LORA_A = 64
LORA_G = 128
MV_LORA = 32
RWKV_GN_EPS = 64e-5
DECAY_SCALE = 0.6065306597126334
D_FF = 2816
CONV_F = 3
ALPHA = (2 * DEPTH) ** 0.25
LN_EPS = 1e-5

LANES = 128
SUBLANES = 8
BF16_ROWS = 16

OFF_XA = 0
OFF_GA = 1024
OFF_R = 2048
OFF_K = 3072
OFF_V = 4096
OFF_GATE = 5120
OFF_LORA = 8192
N_Z = 8448
MAIN_TILE = 2816

FFN_SLABS = ((0, 1536), (1536, 1280))
WKV_CHUNK = 64
WKV_GROUP = 2
RWKV_TILE = 256
ATTN_BLOCK = 128
ATTN_PER_STEP = 8
VMEM_LIMIT = 52 * 1024 * 1024


def _cparams(sem):
    return pltpu.CompilerParams(dimension_semantics=sem, vmem_limit_bytes=VMEM_LIMIT)


def _dot(a, b):
    return jnp.dot(a.astype(BF16), b.astype(BF16), preferred_element_type=F32)


def _dot_nt(a, b):
    return lax.dot_general(a.astype(BF16), b.astype(BF16), (((1,), (1,)), ((), ())),
                           preferred_element_type=F32)


def _dot_tn(a, b):
    return lax.dot_general(a.astype(BF16), b.astype(BF16), (((0,), (0,)), ((), ())),
                           preferred_element_type=F32)


def _split3(x):
    p0 = x.astype(BF16)
    r1 = x - p0.astype(F32)
    p1 = r1.astype(BF16)
    p2 = (r1 - p1.astype(F32)).astype(BF16)
    return p0, p1, p2


def _head_ones():
    i = lax.broadcasted_iota(jnp.int32, (LANES, LANES), 0) // HEAD_DIM
    j = lax.broadcasted_iota(jnp.int32, (LANES, LANES), 1) // HEAD_DIM
    return jnp.where(i == j, 1.0, 0.0).astype(BF16)


def _head_sum(x, ones):
    rows = x.shape[0]
    n = x.shape[1] // LANES
    xs = jnp.concatenate([x[:, c * LANES:(c + 1) * LANES] for c in range(n)], axis=0)
    hi = xs.astype(BF16)
    lo = (xs - hi.astype(F32)).astype(BF16)
    out = jnp.dot(jnp.concatenate([hi, lo], axis=0), ones, preferred_element_type=F32)
    tot = out[:n * rows] + out[n * rows:]
    return jnp.concatenate([tot[c * rows:(c + 1) * rows] for c in range(n)], axis=1)


def _softplus(x):
    return jnp.maximum(x, 0.0) + jnp.log1p(jnp.exp(-jnp.abs(x)))


def _sigmoid(x):
    return 0.5 * jnp.tanh(0.5 * x) + 0.5


def _layer_norm(x, w, b):
    mu = jnp.mean(x, axis=-1, keepdims=True)
    d = x - mu
    var = jnp.mean(d * d, axis=-1, keepdims=True)
    return d * lax.rsqrt(var + LN_EPS) * w + b


def _mod_kernel(c_ref, w_ref, b_ref, o_ref):
    c = c_ref[...]
    s = c * jax.nn.sigmoid(c)
    s0, s1, s2 = _split3(s)
    w0, w1, w2 = _split3(w_ref[...])
    acc = jnp.dot(s0, w0, preferred_element_type=F32)
    acc += jnp.dot(s0, w1, preferred_element_type=F32) + jnp.dot(s1, w0, preferred_element_type=F32)
    acc += (jnp.dot(s1, w1, preferred_element_type=F32) + jnp.dot(s0, w2, preferred_element_type=F32)
            + jnp.dot(s2, w0, preferred_element_type=F32))
    o_ref[...] = acc + b_ref[...]


def _modulation(c, mod_w, mod_b):
    depth, d, n = mod_w.shape
    bsz = c.shape[0]
    rows = BF16_ROWS
    tn = 1536
    c_pad = jnp.pad(c, ((0, rows - bsz), (0, 0)))
    out = pl.pallas_call(
        _mod_kernel,
        grid=(depth, n // tn),
        in_specs=[pl.BlockSpec((rows, d), lambda l, j: (0, 0)),
                  pl.BlockSpec((None, d, tn), lambda l, j: (l, 0, j)),
                  pl.BlockSpec((None, 1, tn), lambda l, j: (l, 0, j))],
        out_specs=pl.BlockSpec((None, rows, tn), lambda l, j: (l, 0, j)),
        out_shape=jax.ShapeDtypeStruct((depth, rows, n), F32),
        compiler_params=_cparams(("arbitrary", "arbitrary")),
        name="adaln_mod",
    )(c_pad, mod_w, mod_b.reshape(depth, 1, n))
    return out[:, :bsz]


def _main_proj_kernel(x_ref, sc_ref, sh_ref, w_ref, z_ref):
    h = (x_ref[...] * (1.0 + sc_ref[...]) + sh_ref[...]).astype(BF16)
    z_ref[...] = jnp.dot(h, w_ref[...], preferred_element_type=F32)


def _main_projection(x, sc, sh, w):
    bsz, s, d = x.shape
    n = w.shape[1]
    tm, tn = 1024, MAIN_TILE
    return pl.pallas_call(
        _main_proj_kernel,
        grid=(n // tn, bsz, s // tm),
        in_specs=[pl.BlockSpec((None, tm, d), lambda j, b, i: (b, i, 0)),
                  pl.BlockSpec((None, 1, d), lambda j, b, i: (b, 0, 0)),
                  pl.BlockSpec((None, 1, d), lambda j, b, i: (b, 0, 0)),
                  pl.BlockSpec((d, tn), lambda j, b, i: (0, j))],
        out_specs=pl.BlockSpec((None, tm, tn), lambda j, b, i: (b, i, j)),
        out_shape=jax.ShapeDtypeStruct((bsz, s, n), F32),
        compiler_params=_cparams(("arbitrary", "arbitrary", "arbitrary")),
        name="main_projection",
    )(x, sc, sh, w)


def _attn_proj_kernel(x_ref, sc_ref, sh_ref, w_ref, cos_ref, sa_ref, sb_ref, *rest, has_vres):
    if has_vres:
        wv_ref, g0_ref, g1_ref, g2_ref, zv_ref, r_scr = rest
    else:
        g0_ref, g1_ref, g2_ref, r_scr = rest
    tm = x_ref.shape[0]
    width = HEADS_PER_GROUP * HEAD_DIM
    half = ROT_DIM // 2
    h = (x_ref[...] * (1.0 + sc_ref[...]) + sh_ref[...]).astype(BF16)
    if has_vres:
        zv_ref[...] = jnp.dot(h, wv_ref[...], preferred_element_type=F32)
    acc = jnp.dot(h, w_ref[...], preferred_element_type=F32)
    n_slab = D_ATTN // LANES
    for sec in range(3):
        scale = HEAD_DIM ** -0.5 if sec == 0 else 1.0
        for c in range(n_slab):
            lo = sec * D_ATTN + c * LANES
            t = acc[:, lo:lo + LANES]
            if sec < 2:
                t = (t * cos_ref[...] + pltpu.roll(t, half, 1) * sa_ref[...]
                     + pltpu.roll(t, LANES - half, 1) * sb_ref[...]) * scale
            r_scr[sec * n_slab + c] = t
    per = width // LANES
    for sec in range(3):
        for g, g_ref in enumerate((g0_ref, g1_ref, g2_ref)):
            dil = g_ref.shape[0]
            for r in range(dil):
                rows = pl.ds(r, tm // dil, stride=dil) if dil > 1 else slice(None)
                for c in range(per):
                    lo = sec * width + c * LANES
                    g_ref[r, :, lo:lo + LANES] = r_scr[sec * n_slab + g * per + c, rows, :].astype(BF16)


def _attn_projection(x, sc, sh, w, w_vres, rope_tabs):
    bsz, s, d = x.shape
    n = w.shape[1]
    tm = 512
    has_vres = w_vres is not None
    tab = pl.BlockSpec((None, tm, LANES), lambda b, i: (b, i, 0))
    in_specs = [pl.BlockSpec((None, tm, d), lambda b, i: (b, i, 0)),
                pl.BlockSpec((None, 1, d), lambda b, i: (b, 0, 0)),
                pl.BlockSpec((None, 1, d), lambda b, i: (b, 0, 0)),
                pl.BlockSpec((d, n), lambda b, i: (0, 0)),
                tab, tab, tab]
    out_specs, out_shape = [], []
    for _, dil in ATTN_GROUPS:
        out_specs.append(pl.BlockSpec((None, dil, tm // dil, D_ATTN), lambda b, i: (b, 0, i, 0)))
        out_shape.append(jax.ShapeDtypeStruct((bsz, dil, s // dil, D_ATTN), BF16))
    args = [x, sc, sh, w, *rope_tabs]
    if has_vres:
        in_specs.append(pl.BlockSpec((d, LANES), lambda b, i: (0, 0)))
        out_specs.append(pl.BlockSpec((None, tm, LANES), lambda b, i: (b, i, 0)))
        out_shape.append(jax.ShapeDtypeStruct((bsz, s, LANES), F32))
        args.append(w_vres)
    outs = pl.pallas_call(
        functools.partial(_attn_proj_kernel, has_vres=has_vres),
        grid=(bsz, s // tm),
        in_specs=in_specs,
        out_specs=out_specs,
        out_shape=out_shape,
        scratch_shapes=[pltpu.VMEM((n // LANES, tm, LANES), F32)],
        compiler_params=_cparams(("arbitrary", "arbitrary")),
        name="attn_projection",
    )(*args)
    return outs[0:3], (outs[3] if has_vres else None)


def _rglru_kernel(xa_ref, ga_ref, cw_ref, vec_ref, wa_ref, wx_ref, y_ref, xbuf, hc):
    t = pl.program_id(2)
    tt, cwid = xa_ref.shape

    @pl.when(t == 0)
    def _():
        xbuf[0:SUBLANES, :] = jnp.zeros((SUBLANES, cwid), F32)
        hc[...] = jnp.zeros_like(hc)

    xa = xa_ref[...]
    xbuf[SUBLANES:SUBLANES + tt, :] = xa
    cw = cw_ref[...]
    vec = vec_ref[...]
    xc = vec[0:1] + cw[0:1] * xa
    for j in range(1, CONV_A):
        xc = xc + cw[j:j + 1] * xbuf[SUBLANES - j:SUBLANES - j + tt, :]
    xbuf[0:SUBLANES, :] = xbuf[tt:tt + SUBLANES, :]

    xb = xc.astype(BF16)
    r = _sigmoid(jnp.dot(xb, wa_ref[...], preferred_element_type=F32) + vec[1:2])
    i = _sigmoid(jnp.dot(xb, wx_ref[...], preferred_element_type=F32) + vec[2:3])
    log_a = (-LRU_C) * r * _softplus(-vec[3:4])
    a = jnp.exp(log_a)
    om = -jnp.tanh(log_a) * (a * a + 1.0)
    b = jnp.where(om > 0.0, om * lax.rsqrt(om), 0.0) * (i * xc)

    ng = tt // SUBLANES
    a = a.reshape(ng, SUBLANES, cwid)
    b = b.reshape(ng, SUBLANES, cwid)
    rows = lax.broadcasted_iota(jnp.int32, (ng, SUBLANES, cwid), 1)
    step = 1
    while step < SUBLANES:
        a_sh = pltpu.roll(a, step, 1)
        b_sh = pltpu.roll(b, step, 1)
        m = rows >= step
        b = jnp.where(m, a * b_sh + b, b)
        a = jnp.where(m, a * a_sh, a)
        step *= 2
    g = ga_ref[...]
    gelu = 0.5 * g * (1.0 + jnp.tanh(0.7978845608028654 * (g + 0.044715 * (g * g * g))))
    carry = hc[0:1, :]
    hs = []
    for i in range(ng):
        h = a[i] * carry + b[i]
        carry = h[SUBLANES - 1:SUBLANES, :]
        hs.append(h)
    hc[0:1, :] = carry
    y_ref[...] = (jnp.concatenate(hs, axis=0) * gelu).astype(BF16)


def _rglru(z, conv_w, vec, wa_bd, wx_bd):
    bsz, s, _ = z.shape
    tt, cwid = 1024, 256
    nc = D_RNN // cwid
    return pl.pallas_call(
        _rglru_kernel,
        grid=(bsz, nc, s // tt),
        in_specs=[pl.BlockSpec((None, tt, cwid), lambda b, c, t: (b, t, OFF_XA // cwid + c)),
                  pl.BlockSpec((None, tt, cwid), lambda b, c, t: (b, t, OFF_GA // cwid + c)),
                  pl.BlockSpec((CONV_A, cwid), lambda b, c, t: (0, c)),
                  pl.BlockSpec((SUBLANES, cwid), lambda b, c, t: (0, c)),
                  pl.BlockSpec((None, cwid, cwid), lambda b, c, t: (c, 0, 0)),
                  pl.BlockSpec((None, cwid, cwid), lambda b, c, t: (c, 0, 0))],
        out_specs=pl.BlockSpec((None, tt, cwid), lambda b, c, t: (b, t, c)),
        out_shape=jax.ShapeDtypeStruct((bsz, s, D_RNN), BF16),
        scratch_shapes=[pltpu.VMEM((tt + SUBLANES, cwid), F32), pltpu.VMEM((SUBLANES, cwid), F32)],
        compiler_params=_cparams(("arbitrary", "arbitrary", "arbitrary")),
        name="rglru",
    )(z, z, conv_w, vec, wa_bd, wx_bd)


def _rope_table_kernel(pos_ref, freq_ref, cos_ref, sa_ref, sb_ref):
    tt = pos_ref.shape[0]
    ang = pos_ref[...].astype(F32) * freq_ref[...]
    lane = lax.broadcasted_iota(jnp.int32, (tt, LANES), 1) % HEAD_DIM
    sn = jnp.sin(ang)
    half = ROT_DIM // 2
    cos_ref[...] = jnp.cos(ang)
    sa_ref[...] = jnp.where((lane >= half) & (lane < ROT_DIM), sn, 0.0)
    sb_ref[...] = jnp.where(lane < half, -sn, 0.0)


def _rope_tables(pos3, freq_lane):
    bsz, s, _ = pos3.shape
    tt = 1024
    out = pl.BlockSpec((None, tt, LANES), lambda b, t: (b, t, 0))
    return pl.pallas_call(
        _rope_table_kernel,
        grid=(bsz, s // tt),
        in_specs=[pl.BlockSpec((None, tt, 1), lambda b, t: (b, t, 0)),
                  pl.BlockSpec((1, LANES), lambda b, t: (0, 0))],
        out_specs=[out] * 3,
        out_shape=[jax.ShapeDtypeStruct((bsz, s, LANES), F32)] * 3,
        compiler_params=_cparams(("arbitrary", "arbitrary")),
        name="rope_tables",
    )(pos3, freq_lane)


def _attn_kernel(q_ref, kp_ref, kc_ref, vp_ref, vc_ref, o_ref, lse_ref):
    nq = ATTN_BLOCK
    n_res, n_rows, width = q_ref.shape
    nh = HEADS_PER_GROUP
    nk = 2 * nq
    head = lax.broadcasted_iota(jnp.int32, (nq, width), 1) // HEAD_DIM
    qi = lax.broadcasted_iota(jnp.int32, (nh * nq, nk), 0) % nq
    kj = lax.broadcasted_iota(jnp.int32, (nh * nq, nk), 1)
    band = (kj >= qi) & (kj <= qi + nq)
    for rr in range(n_res):
        k_all = jnp.concatenate([kp_ref[rr], kc_ref[rr]], axis=0)
        v_all = jnp.concatenate([vp_ref[rr], vc_ref[rr]], axis=0)
        for j in range(n_rows // nq):
            q = q_ref[rr, j * nq:(j + 1) * nq, :]
            k = k_all[j * nq:j * nq + nk]
            v = v_all[j * nq:j * nq + nk]
            qcat = jnp.concatenate([jnp.where(head == h, q, jnp.zeros_like(q))
                                    for h in range(nh)], axis=0)
            s = lax.dot_general(qcat, k, (((1,), (1,)), ((), ())), preferred_element_type=F32)
            if j == 0:
                valid = band & ((pl.program_id(2) > 0) | (kj >= nq))
            else:
                valid = band
            s = jnp.where(valid, s, -1e30)
            m = jnp.max(s, axis=1, keepdims=True)
            p = jnp.exp(s - m)
            l = jnp.sum(p, axis=1, keepdims=True)
            pv = jnp.dot(p.astype(BF16), v, preferred_element_type=F32) / l
            lse_rows = m + jnp.log(l)
            o = jnp.zeros((nq, width), F32)
            lse = jnp.zeros((nq, width), F32)
            for h in range(nh):
                hm = head == h
                rows = slice(h * nq, (h + 1) * nq)
                o = jnp.where(hm, pv[rows], o)
                lse = jnp.where(hm, lse_rows[rows], lse)
            o_ref[rr, j * nq:(j + 1) * nq, :] = o
            lse_ref[rr, j * nq:(j + 1) * nq, :] = lse


def _attention_group(qkv, g):
    bsz, dil, sub, _ = qkv.shape
    width = HEADS_PER_GROUP * HEAD_DIM
    nb = ATTN_BLOCK

    per = min(ATTN_PER_STEP, sub // nb)
    n_res = min(dil, ATTN_PER_STEP // per)

    def cur(sec):
        return lambda b, r, m: (b, r, m, sec)

    def prev(sec):
        return lambda b, r, m: (b, r, jnp.maximum(m * per - 1, 0), sec)

    blk = (None, n_res, per * nb, width)
    pblk = (None, n_res, nb, width)
    return pl.pallas_call(
        _attn_kernel,
        grid=(bsz, dil // n_res, sub // (per * nb)),
        in_specs=[pl.BlockSpec(blk, cur(0)),
                  pl.BlockSpec(pblk, prev(1)), pl.BlockSpec(blk, cur(1)),
                  pl.BlockSpec(pblk, prev(2)), pl.BlockSpec(blk, cur(2))],
        out_specs=[pl.BlockSpec(blk, cur(0)), pl.BlockSpec(blk, cur(0))],
        out_shape=[jax.ShapeDtypeStruct((bsz, dil, sub, width), F32)] * 2,
        compiler_params=_cparams(("arbitrary", "arbitrary", "arbitrary")),
        name=f"dilated_attention_g{g}",
    )(qkv, qkv, qkv, qkv, qkv)


def _token_shift(z, halo, first, mu):
    ext = jnp.concatenate([jnp.where(first, 0.0, halo), z], axis=0)
    zp = pltpu.roll(ext, 1, 0)[SUBLANES:]
    return z + (zp - z) * mu


def _wkv_tile(r, k, v, a, b, lw, s_scr):
    L = WKV_CHUNK
    n_rows, width = r.shape
    n_chunks = n_rows // L
    n_pairs = width // LANES
    ti = lax.broadcasted_iota(jnp.int32, (n_rows, n_rows), 0)
    tj = lax.broadcasted_iota(jnp.int32, (n_rows, n_rows), 1)
    tri = jnp.where((ti >= tj) & ((ti // L) == (tj // L)), 1.0, 0.0).astype(BF16)
    l0, l1, l2 = _split3(lw)
    lc = (jnp.dot(tri, l0, preferred_element_type=F32) + jnp.dot(tri, l1, preferred_element_type=F32)
          + jnp.dot(tri, l2, preferred_element_type=F32))
    lc_ends = [lc[(c + 1) * L - 1:(c + 1) * L, :] for c in range(n_chunks)]
    lc_last = jnp.concatenate([jnp.broadcast_to(e, (L, width)) for e in lc_ends], axis=0)
    e_inv = jnp.exp(-lc)
    e_tail = jnp.exp(lc_last - lc)
    a_hat = a * jnp.exp(lc - lw)
    r_hat = r * jnp.exp(lc)
    b_hat = b * e_inv
    k_hat = k * e_inv
    b_til = b * e_tail
    k_til = k * e_tail
    d_last = [jnp.exp(e) for e in lc_ends]

    even = lax.broadcasted_iota(jnp.int32, (L, LANES), 1) < HEAD_DIM

    def cat(x):
        x = x.astype(BF16)
        zero = jnp.zeros_like(x)
        return jnp.concatenate([jnp.where(even, x, zero), jnp.where(even, zero, x)], axis=0)

    def stack(*xs):
        return jnp.concatenate([x.astype(BF16) for x in xs], axis=0)

    ti = lax.broadcasted_iota(jnp.int32, (L, 2 * L), 0)
    tj = lax.broadcasted_iota(jnp.int32, (L, 2 * L), 1) % L
    strict = ti > tj
    incl = ti >= tj
    eye = jnp.where(ti == tj, 1.0, 0.0)
    pi = lax.broadcasted_iota(jnp.int32, (LANES, LANES), 0)
    qi = lax.broadcasted_iota(jnp.int32, (LANES, LANES), 1)
    same_head = (pi // HEAD_DIM) == (qi // HEAD_DIM)

    pairs = range(n_pairs)
    qm, c_mat, y0 = [], [], []
    for c0 in range(0, n_chunks, WKV_GROUP):
        where = [(slice(c * L, (c + 1) * L), slice(p * LANES, (p + 1) * LANES))
                 for c in range(c0, c0 + WKV_GROUP) for p in pairs]
        items = range(len(where))
        v_c = [cat(v[w]) for w in where]
        gram = [_dot_nt(stack(a_hat[w], r_hat[w]),
                        jnp.concatenate([cat(b_hat[w]), cat(k_hat[w])], axis=0))
                for w in where]
        nmat = [jnp.where(strict, gram[i][:L, :LANES], 0.0) for i in items]
        a_ak = [jnp.where(strict, gram[i][:L, LANES:], 0.0) for i in items]
        a_rb = [jnp.where(incl, gram[i][L:, :LANES], 0.0) for i in items]
        a_rk = [jnp.where(incl, gram[i][L:, LANES:], 0.0) for i in items]
        av = [_dot(stack(a_ak[i], a_rk[i]), v_c[i]) for i in items]
        tinv = [eye + nmat[i] for i in items]
        npow = [_dot(nmat[i], cat(nmat[i])) for i in items]
        span = 4
        while span < L:
            prod = [_dot(stack(tinv[i], npow[i]), cat(npow[i])) for i in items]
            tinv = [tinv[i] + prod[i][:L] for i in items]
            npow = [prod[i][L:] for i in items]
            span *= 2
        tinv = [tinv[i] + _dot(tinv[i], cat(npow[i])) for i in items]
        wu = [_dot(tinv[i], jnp.concatenate([cat(a_hat[where[i]]), cat(av[i][:L])], axis=1))
              for i in items]
        qy = [_dot(a_rb[i], jnp.concatenate([cat(wu[i][:, :LANES]), cat(wu[i][:, LANES:])],
                                            axis=1)) for i in items]
        mc = [_dot_tn(stack(b_til[where[i]], k_til[where[i]]),
                      jnp.concatenate([wu[i].astype(BF16),
                                       jnp.concatenate([jnp.zeros((L, LANES), BF16),
                                                        v[where[i]].astype(BF16)], axis=1)],
                                      axis=0))
              for i in items]
        for i in items:
            c = c0 + i // n_pairs
            dl = jnp.broadcast_to(d_last[c][:, where[i][1]], (LANES, LANES))
            m_mat = jnp.where(same_head, mc[i][:, :LANES], 0.0) + jnp.where(pi == qi, dl, 0.0)
            c_mat.append(jnp.where(same_head, mc[i][:, LANES:], 0.0))
            qm.append(stack(r_hat[where[i]] + qy[i][:, :LANES], m_mat))
            y0.append(qy[i][:, LANES:] + av[i][L:])

    state = [s_scr[p] for p in pairs]
    rows = []
    for c in range(n_chunks):
        ys = []
        for p in pairs:
            i = c * n_pairs + p
            out = _dot(qm[i], state[p])
            ys.append(out[:L] + y0[i])
            state[p] = out[L:] + c_mat[i]
        rows.append(jnp.concatenate(ys, axis=1))
    for p in pairs:
        s_scr[p] = state[p]
    return jnp.concatenate(rows, axis=0)


def _rwkv_kernel(*refs, has_vres):
    if has_vres:
        (zr_ref, zk_ref, zv_ref, zl_ref, hr_ref, hk_ref, hv_ref, hl_ref, vec_ref, mul_ref,
         w2_ref, a2_ref, g2_ref, zv1_ref, hv1_ref, vf_ref, muv_ref, v2_ref, y_ref, s_scr) = refs
    else:
        (zr_ref, zk_ref, zv_ref, zl_ref, hr_ref, hk_ref, hv_ref, hl_ref, vec_ref, mul_ref,
         w2_ref, a2_ref, g2_ref, y_ref, vf_out, s_scr) = refs
    first = pl.program_id(1) == 0

    @pl.when(first)
    def _():
        s_scr[...] = jnp.zeros_like(s_scr)

    vec = vec_ref[...]
    r = _token_shift(zr_ref[...], hr_ref[...], first, vec[0:1])
    k = _token_shift(zk_ref[...], hk_ref[...], first, vec[1:2])
    v = _token_shift(zv_ref[...], hv_ref[...], first, vec[2:3])
    lora = _token_shift(zl_ref[...], hl_ref[...], first, mul_ref[...])
    wa_in = lora[:, 0:LANES]
    w_pre = vec[3:4] + _dot(jnp.tanh(wa_in), w2_ref[...])
    lw = (-DECAY_SCALE) * _sigmoid(w_pre)
    alpha = _sigmoid(vec[4:5] + _dot(wa_in, a2_ref[...]))
    g = _dot(_sigmoid(lora[:, LANES:2 * LANES]), g2_ref[...])
    if has_vres:
        zv1 = _token_shift(zv1_ref[...], hv1_ref[...], first, muv_ref[...])
        mix = _sigmoid(vec[7:8] + _dot(zv1, v2_ref[...]))
        v = v + (vf_ref[...] - v) * mix
    else:
        vf_out[...] = v
    ones = _head_ones()
    kk = k * vec[5:6]
    kk = kk * lax.rsqrt(_head_sum(kk * kk, ones) + 1e-12)
    k = k * (1.0 + (alpha - 1.0) * vec[6:7])
    a = -kk
    b = kk * alpha

    y = _wkv_tile(r, k, v, a, b, lw, s_scr)

    inv_n = 1.0 / HEAD_DIM
    mu = _head_sum(y, ones) * inv_n
    dy = y - mu
    var = _head_sum(dy * dy, ones) * inv_n
    gn = dy * lax.rsqrt(var + RWKV_GN_EPS) * vec[9:10] + vec[10:11]
    bonus = _head_sum(r * k * vec[8:9], ones) * v
    y_ref[...] = ((gn + bonus) * g).astype(BF16)


def _rwkv(z, zv1, v_first, vec, mu_l, w2p, a2p, g2, mu_v1, v2p):
    bsz, s, _ = z.shape
    tt = RWKV_TILE
    d = D_RWKV
    has_vres = zv1 is not None
    hstep = tt // SUBLANES

    def col(off, w):
        return lambda b, t: (b, t, off // w)

    def hcol(off, w):
        return lambda b, t: (b, jnp.maximum(t * hstep - 1, 0), off // w)

    lw = 2 * LANES
    in_specs = [pl.BlockSpec((None, tt, d), col(OFF_R, d)),
                pl.BlockSpec((None, tt, d), col(OFF_K, d)),
                pl.BlockSpec((None, tt, d), col(OFF_V, d)),
                pl.BlockSpec((None, tt, lw), col(OFF_LORA, lw)),
                pl.BlockSpec((None, SUBLANES, d), hcol(OFF_R, d)),
                pl.BlockSpec((None, SUBLANES, d), hcol(OFF_K, d)),
                pl.BlockSpec((None, SUBLANES, d), hcol(OFF_V, d)),
                pl.BlockSpec((None, SUBLANES, lw), hcol(OFF_LORA, lw)),
                pl.BlockSpec(vec.shape, lambda b, t: (0, 0)),
                pl.BlockSpec((1, lw), lambda b, t: (0, 0)),
                pl.BlockSpec((LANES, d), lambda b, t: (0, 0)),
                pl.BlockSpec((LANES, d), lambda b, t: (0, 0)),
                pl.BlockSpec((LANES, d), lambda b, t: (0, 0))]
    args = [z, z, z, z, z, z, z, z, vec, mu_l, w2p, a2p, g2]
    row = pl.BlockSpec((None, tt, d), lambda b, t: (b, t, 0))
    out_specs = [row]
    out_shape = [jax.ShapeDtypeStruct((bsz, s, d), BF16)]
    if has_vres:
        in_specs += [pl.BlockSpec((None, tt, LANES), lambda b, t: (b, t, 0)),
                     pl.BlockSpec((None, SUBLANES, LANES),
                                  lambda b, t: (b, jnp.maximum(t * hstep - 1, 0), 0)),
                     row,
                     pl.BlockSpec((1, LANES), lambda b, t: (0, 0)),
                     pl.BlockSpec((LANES, d), lambda b, t: (0, 0))]
        args += [zv1, zv1, v_first, mu_v1, v2p]
    else:
        out_specs.append(row)
        out_shape.append(jax.ShapeDtypeStruct((bsz, s, d), F32))
    outs = pl.pallas_call(
        functools.partial(_rwkv_kernel, has_vres=has_vres),
        grid=(bsz, s // tt),
        in_specs=in_specs,
        out_specs=out_specs,
        out_shape=out_shape,
        scratch_shapes=[pltpu.VMEM((d // LANES, LANES, LANES), F32)],
        compiler_params=_cparams(("arbitrary", "arbitrary")),
        name="rwkv7",
    )(*args)
    return outs[0], (v_first if has_vres else outs[1])


def _interleave(ref, scr):
    dil, n, w = ref.shape
    if dil == 1:
        return ref[0]
    for r in range(dil):
        for c in range(w // LANES):
            scr[c, pl.ds(r, n, stride=dil), :] = ref[r, :, c * LANES:(c + 1) * LANES]
    return jnp.concatenate([scr[c] for c in range(w // LANES)], axis=1)


def _merge_kernel(x_ref, ya_ref, yc_ref, o0_ref, o1_ref, o2_ref, l0_ref, l1_ref, l2_ref,
                  ga_ref, gb_ref, gc_ref, gt_ref, pa_ref, pb_ref, pc_ref, wo_ref, ln_ref, out_ref,
                  so1, so2, sl1, sl2):
    l0 = l0_ref[0]
    l1 = _interleave(l1_ref, sl1)
    l2 = _interleave(l2_ref, sl2)
    o0 = o0_ref[0]
    o1 = _interleave(o1_ref, so1)
    o2 = _interleave(o2_ref, so2)
    lm = jnp.maximum(jnp.maximum(l0, l1), l2)
    e0 = jnp.exp(l0 - lm)
    e1 = jnp.exp(l1 - lm)
    e2 = jnp.exp(l2 - lm)
    yb = ((e0 * o0 + e1 * o1 + e2 * o2) / (e0 + e1 + e2)).astype(BF16)
    ln = ln_ref[...]
    gt = 1.0 + gt_ref[...]
    n_half = 2
    hrows = x_ref.shape[0] // n_half
    for hf in range(n_half):
        rows = slice(hf * hrows, (hf + 1) * hrows)
        merged = (jax.nn.sigmoid(ga_ref[rows, :])
                  * jnp.dot(ya_ref[rows, :], pa_ref[...], preferred_element_type=F32)
                  + jax.nn.sigmoid(gb_ref[rows, :])
                  * jnp.dot(yb[rows], pb_ref[...], preferred_element_type=F32)
                  + jax.nn.sigmoid(gc_ref[rows, :])
                  * jnp.dot(yc_ref[rows, :], pc_ref[...], preferred_element_type=F32))
        y = _dot(merged, wo_ref[...])
        out_ref[rows, :] = _layer_norm(ALPHA * x_ref[rows, :] + gt * y, ln[0:1], ln[1:2])


def _merge(x, z, ya, yc, att, gt, pa, pb, pc, wo, ln):
    bsz, s, d = x.shape
    tm = 512
    wa = D_ATTN_OUT
    row = lambda w, off=0: pl.BlockSpec((None, tm, w), lambda b, i: (b, i, off // w))
    const = lambda shp: pl.BlockSpec(shp, lambda b, i: (0,) * len(shp))
    res = lambda dil: pl.BlockSpec((None, dil, tm // dil, wa), lambda b, i: (b, 0, i, 0))
    (o0, l0), (o1, l1), (o2, l2) = att
    att_specs = [res(dil) for _, dil in ATTN_GROUPS] * 2
    return pl.pallas_call(
        _merge_kernel,
        grid=(bsz, s // tm),
        scratch_shapes=[pltpu.VMEM((wa // LANES, tm, LANES), F32)] * 4,
        in_specs=[row(d), row(d), row(d)] + att_specs
                 + [row(d, OFF_GATE), row(d, OFF_GATE + d), row(d, OFF_GATE + 2 * d),
                    pl.BlockSpec((None, 1, d), lambda b, i: (b, 0, 0)),
                    const((d, d)), const((wa, d)), const((d, d)), const((d, d)),
                    const((SUBLANES, d))],
        out_specs=row(d),
        out_shape=jax.ShapeDtypeStruct((bsz, s, d), F32),
        compiler_params=_cparams(("arbitrary", "arbitrary")),
        name="merge_proj_ln",
    )(x, ya, yc, o0, o1, o2, l0, l1, l2, z, z, z, gt, pa, pb, pc, wo, ln)


def _ffn_kernel(x_ref, halo_ref, sc_ref, sh_ref, gt_ref, wu_ref, cw_ref, wd_ref, ln_ref, out_ref):
    i = pl.program_id(1)
    hr = halo_ref.shape[0]
    sc = 1.0 + sc_ref[...]
    sh = sh_ref[...]
    x = x_ref[...]
    hh = jnp.where(i == 0, 0.0, halo_ref[...] * sc + sh)
    h = jnp.concatenate([hh.astype(BF16), (x * sc + sh).astype(BF16)], axis=0)

    def up(c):
        lo, n = FFN_SLABS[c]
        return [jnp.dot(h, wu_ref[:, off:off + n], preferred_element_type=F32)
                for off in (lo, D_FF + lo)]

    def conv(u, off):
        cw = cw_ref[:, off:off + u.shape[1]]
        out = cw[CONV_F:CONV_F + 1] + cw[0:1] * u[hr:]
        for j in range(1, CONV_F):
            out = out + cw[j:j + 1] * pltpu.roll(u, j, 0)[hr:]
        return out

    acc = None
    u_next = up(0)
    for c, (lo, n) in enumerate(FFN_SLABS):
        ug, uv = u_next
        if c + 1 < len(FFN_SLABS):
            u_next = up(c + 1)
        cg = conv(ug, lo)
        cv = conv(uv, D_FF + lo)
        act = (cg * _sigmoid(cg) * cv).astype(BF16)
        part = jnp.dot(act, wd_ref[lo:lo + n, :], preferred_element_type=F32)
        acc = part if acc is None else acc + part
    ln = ln_ref[...]
    out_ref[...] = _layer_norm(ALPHA * x + (1.0 + gt_ref[...]) * acc, ln[0:1], ln[1:2])


def _conv_ffn(x, sc, sh, gt, w_up, conv_wb, w_down, ln):
    bsz, s, d = x.shape
    tm = 512
    hr = BF16_ROWS
    mod = pl.BlockSpec((None, 1, d), lambda b, i: (b, 0, 0))
    once = pl.Buffered(1)
    return pl.pallas_call(
        _ffn_kernel,
        grid=(bsz, s // tm),
        in_specs=[pl.BlockSpec((None, tm, d), lambda b, i: (b, i, 0)),
                  pl.BlockSpec((None, hr, d), lambda b, i: (b, jnp.maximum(i * (tm // hr) - 1, 0), 0)),
                  mod, mod, mod,
                  pl.BlockSpec(w_up.shape, lambda b, i: (0, 0), pipeline_mode=once),
                  pl.BlockSpec(conv_wb.shape, lambda b, i: (0, 0), pipeline_mode=once),
                  pl.BlockSpec(w_down.shape, lambda b, i: (0, 0), pipeline_mode=once),
                  pl.BlockSpec((SUBLANES, d), lambda b, i: (0, 0))],
        out_specs=pl.BlockSpec((None, tm, d), lambda b, i: (b, i, 0)),
        out_shape=jax.ShapeDtypeStruct((bsz, s, d), F32),
        compiler_params=_cparams(("arbitrary", "arbitrary")),
        name="conv_ffn_ln",
    )(x, x, sc, sh, gt, w_up, conv_wb, w_down, ln)


def _rows(vectors, n_rows=SUBLANES):
    m = jnp.stack([v.astype(F32) for v in vectors], axis=0)
    return jnp.pad(m, ((0, n_rows - m.shape[0]), (0, 0)))


def _block_diag(w, per):
    g, n, _ = w.shape
    w = w.reshape(g // per, per, n, n)
    eye = jnp.eye(per, dtype=w.dtype)
    out = jnp.einsum('cpij,pq->cpiqj', w, eye)
    return out.reshape(g // per, per * n, per * n)


def kernel(x, c, positions, mod_w, mod_b, w_in, w_in_vres, conv_a_w, conv_a_b, lru_wa, lru_ba, lru_wx, lru_bx, lru_lambda, rwkv_mu, mu_vres, w0, w2, a0, a2, g2, v0, v2, k_k, k_a, r_k, ln_x_w, ln_x_b, proj_a, proj_b, proj_c, w_o, ln1_w, ln1_b, ffn_up, ffn_conv_w, ffn_conv_b, ffn_down, ln2_w, ln2_b):
    bsz, s, d = x.shape
    mod = _modulation(c, mod_w, mod_b)
    half = ROT_DIM // 2
    inv_freq = ROPE_THETA ** (-jnp.arange(half, dtype=F32) / half)
    lane = jnp.arange(LANES)
    freq_lane = jnp.where((lane % HEAD_DIM) < ROT_DIM, inv_freq[lane % half], 0.0).reshape(1, LANES)
    rope_tabs = _rope_tables(positions.reshape(bsz, s, 1), freq_lane)
    zero_d = jnp.zeros((d,), F32)

    v_first = None
    for l in range(DEPTH):
        m6 = mod[l].reshape(bsz, 6, 1, d)
        sh1, sc1, gt1, sh2, sc2, gt2 = (m6[:, i] for i in range(6))

        wl = w_in[l]
        w_main = jnp.concatenate([wl[:, 0:2048], wl[:, 4352:7424], wl[:, 7680:10752],
                                  wl[:, 7424:7680]], axis=1).astype(BF16)
        if l == 0:
            w_vres = None
        else:
            w_vres = jnp.pad(w_in_vres[l - 1], ((0, 0), (0, LANES - MV_LORA))).astype(BF16)
        z = _main_projection(x, sc1, sh1, w_main)
        qkv_groups, zv1 = _attn_projection(x, sc1, sh1, wl[:, 2048:4352].astype(BF16), w_vres,
                                           rope_tabs)

        per = 256 // RNN_BLOCK
        y_a = _rglru(z, conv_a_w[l],
                     _rows([conv_a_b[l], lru_ba[l], lru_bx[l], lru_lambda[l]]),
                     _block_diag(lru_wa[l], per).astype(BF16),
                     _block_diag(lru_wx[l], per).astype(BF16))

        att = [_attention_group(qkv_groups[g], g) for g in range(N_GROUPS)]

        mu = rwkv_mu[l]
        vec = _rows([mu[0:1024], mu[1024:2048], mu[2048:3072], w0[l], a0[l], k_k[l], k_a[l],
                     v0[l - 1] if l > 0 else zero_d, r_k[l].reshape(-1), ln_x_w[l], ln_x_b[l]],
                    n_rows=2 * SUBLANES)
        mu_l = mu[3072:3328].reshape(1, 2 * LANES)
        w2p = jnp.pad(w2[l], ((0, LANES - LORA_W), (0, 0))).astype(BF16)
        a2p = jnp.pad(a2[l], ((LORA_W, 0), (0, 0))).astype(BF16)
        if l == 0:
            y_c, v_first = _rwkv(z, None, None, vec, mu_l, w2p, a2p, g2[l].astype(BF16), None, None)
        else:
            mu_v1 = jnp.pad(mu_vres[l - 1], (0, LANES - MV_LORA)).reshape(1, LANES)
            v2p = jnp.pad(v2[l - 1], ((0, LANES - MV_LORA), (0, 0))).astype(BF16)
            y_c, _ = _rwkv(z, zv1, v_first, vec, mu_l, w2p, a2p, g2[l].astype(BF16), mu_v1, v2p)

        x = _merge(x, z, y_a, y_c, att, gt1, proj_a[l].astype(BF16), proj_b[l].astype(BF16),
                   proj_c[l].astype(BF16), w_o[l].astype(BF16), _rows([ln1_w[l], ln1_b[l]]))

        conv_wb = jnp.concatenate([ffn_conv_w[l], ffn_conv_b[l][None, :]], axis=0)
        conv_wb = jnp.pad(conv_wb, ((0, SUBLANES - conv_wb.shape[0]), (0, 0)))
        x = _conv_ffn(x, sc2, sh2, gt2, ffn_up[l].astype(BF16), conv_wb,
                      ffn_down[l].astype(BF16), _rows([ln2_w[l], ln2_b[l]]))
    return x
```

```python
import functools

import jax
import jax.numpy as jnp
from jax import lax
from jax.experimental import pallas as pl
from jax.experimental.pallas import tpu as pltpu

F32 = jnp.float32
BF16 = jnp.bfloat16

D_MODEL = 1024
DEPTH = 2
D_RNN = 1024
RNN_BLOCK = 64
CONV_A = 4
LRU_C = 8.0
HEAD_DIM = 64
ATTN_GROUPS = ((128, 1), (512, 4), (2048, 16))
HEADS_PER_GROUP = 4
N_GROUPS = 3
D_ATTN = 768
D_ATTN_OUT = 256
ROT_DIM = 16
ROPE_THETA = 500000.0
D_RWKV = 1024
LORA_W = 64
LORA_A = 64
LORA_G = 128
MV_LORA = 32
RWKV_GN_EPS = 64e-5
DECAY_SCALE = 0.6065306597126334
D_FF = 2816
CONV_F = 3
ALPHA = (2 * DEPTH) ** 0.25
LN_EPS = 1e-5

LANES = 128
SUBLANES = 8
BF16_ROWS = 16

OFF_XA = 0
OFF_GA = 1024
OFF_R = 2048
OFF_K = 3072
OFF_V = 4096
OFF_GATE = 5120
OFF_LORA = 8192
N_Z = 8448
MAIN_TILE = 2816

FFN_SLABS = ((0, 1536), (1536, 1280))
WKV_CHUNK = 64
WKV_GROUP = 2
RWKV_TILE = 256
ATTN_BLOCK = 128
ATTN_PER_STEP = 8
VMEM_LIMIT = 52 * 1024 * 1024


def _cparams(sem):
    return pltpu.CompilerParams(dimension_semantics=sem, vmem_limit_bytes=VMEM_LIMIT)


def _dot(a, b):
    return jnp.dot(a.astype(BF16), b.astype(BF16), preferred_element_type=F32)


def _dot_nt(a, b):
    return lax.dot_general(a.astype(BF16), b.astype(BF16), (((1,), (1,)), ((), ())),
                           preferred_element_type=F32)


def _dot_tn(a, b):
    return lax.dot_general(a.astype(BF16), b.astype(BF16), (((0,), (0,)), ((), ())),
                           preferred_element_type=F32)


def _split3(x):
    p0 = x.astype(BF16)
    r1 = x - p0.astype(F32)
    p1 = r1.astype(BF16)
    p2 = (r1 - p1.astype(F32)).astype(BF16)
    return p0, p1, p2


def _head_ones():
    i = lax.broadcasted_iota(jnp.int32, (LANES, LANES), 0) // HEAD_DIM
    j = lax.broadcasted_iota(jnp.int32, (LANES, LANES), 1) // HEAD_DIM
    return jnp.where(i == j, 1.0, 0.0).astype(BF16)


def _head_sum(x, ones):
    rows = x.shape[0]
    n = x.shape[1] // LANES
    xs = jnp.concatenate([x[:, c * LANES:(c + 1) * LANES] for c in range(n)], axis=0)
    hi = xs.astype(BF16)
    lo = (xs - hi.astype(F32)).astype(BF16)
    out = jnp.dot(jnp.concatenate([hi, lo], axis=0), ones, preferred_element_type=F32)
    tot = out[:n * rows] + out[n * rows:]
    return jnp.concatenate([tot[c * rows:(c + 1) * rows] for c in range(n)], axis=1)


def _softplus(x):
    return jnp.maximum(x, 0.0) + jnp.log1p(jnp.exp(-jnp.abs(x)))


def _sigmoid(x):
    return 0.5 * jnp.tanh(0.5 * x) + 0.5


def _layer_norm(x, w, b):
    mu = jnp.mean(x, axis=-1, keepdims=True)
    d = x - mu
    var = jnp.mean(d * d, axis=-1, keepdims=True)
    return d * lax.rsqrt(var + LN_EPS) * w + b


def _mod_kernel(c_ref, w_ref, b_ref, o_ref):
    c = c_ref[...]
    s = c * jax.nn.sigmoid(c)
    s0, s1, s2 = _split3(s)
    w0, w1, w2 = _split3(w_ref[...])
    acc = jnp.dot(s0, w0, preferred_element_type=F32)
    acc += jnp.dot(s0, w1, preferred_element_type=F32) + jnp.dot(s1, w0, preferred_element_type=F32)
    acc += (jnp.dot(s1, w1, preferred_element_type=F32) + jnp.dot(s0, w2, preferred_element_type=F32)
            + jnp.dot(s2, w0, preferred_element_type=F32))
    o_ref[...] = acc + b_ref[...]


def _modulation(c, mod_w, mod_b):
    depth, d, n = mod_w.shape
    bsz = c.shape[0]
    rows = BF16_ROWS
    tn = 1536
    c_pad = jnp.pad(c, ((0, rows - bsz), (0, 0)))
    out = pl.pallas_call(
        _mod_kernel,
        grid=(depth, n // tn),
        in_specs=[pl.BlockSpec((rows, d), lambda l, j: (0, 0)),
                  pl.BlockSpec((None, d, tn), lambda l, j: (l, 0, j)),
                  pl.BlockSpec((None, 1, tn), lambda l, j: (l, 0, j))],
        out_specs=pl.BlockSpec((None, rows, tn), lambda l, j: (l, 0, j)),
        out_shape=jax.ShapeDtypeStruct((depth, rows, n), F32),
        compiler_params=_cparams(("arbitrary", "arbitrary")),
        name="adaln_mod",
    )(c_pad, mod_w, mod_b.reshape(depth, 1, n))
    return out[:, :bsz]


def _main_proj_kernel(x_ref, sc_ref, sh_ref, w_ref, z_ref):
    h = (x_ref[...] * (1.0 + sc_ref[...]) + sh_ref[...]).astype(BF16)
    z_ref[...] = jnp.dot(h, w_ref[...], preferred_element_type=F32)


def _main_projection(x, sc, sh, w):
    bsz, s, d = x.shape
    n = w.shape[1]
    tm, tn = 1024, MAIN_TILE
    return pl.pallas_call(
        _main_proj_kernel,
        grid=(n // tn, bsz, s // tm),
        in_specs=[pl.BlockSpec((None, tm, d), lambda j, b, i: (b, i, 0)),
                  pl.BlockSpec((None, 1, d), lambda j, b, i: (b, 0, 0)),
                  pl.BlockSpec((None, 1, d), lambda j, b, i: (b, 0, 0)),
                  pl.BlockSpec((d, tn), lambda j, b, i: (0, j))],
        out_specs=pl.BlockSpec((None, tm, tn), lambda j, b, i: (b, i, j)),
        out_shape=jax.ShapeDtypeStruct((bsz, s, n), F32),
        compiler_params=_cparams(("arbitrary", "arbitrary", "arbitrary")),
        name="main_projection",
    )(x, sc, sh, w)


def _attn_proj_kernel(x_ref, sc_ref, sh_ref, w_ref, cos_ref, sa_ref, sb_ref, *rest, has_vres):
    if has_vres:
        wv_ref, g0_ref, g1_ref, g2_ref, zv_ref, r_scr = rest
    else:
        g0_ref, g1_ref, g2_ref, r_scr = rest
    tm = x_ref.shape[0]
    width = HEADS_PER_GROUP * HEAD_DIM
    half = ROT_DIM // 2
    h = (x_ref[...] * (1.0 + sc_ref[...]) + sh_ref[...]).astype(BF16)
    if has_vres:
        zv_ref[...] = jnp.dot(h, wv_ref[...], preferred_element_type=F32)
    acc = jnp.dot(h, w_ref[...], preferred_element_type=F32)
    n_slab = D_ATTN // LANES
    for sec in range(3):
        scale = HEAD_DIM ** -0.5 if sec == 0 else 1.0
        for c in range(n_slab):
            lo = sec * D_ATTN + c * LANES
            t = acc[:, lo:lo + LANES]
            if sec < 2:
                t = (t * cos_ref[...] + pltpu.roll(t, half, 1) * sa_ref[...]
                     + pltpu.roll(t, LANES - half, 1) * sb_ref[...]) * scale
            r_scr[sec * n_slab + c] = t
    per = width // LANES
    for sec in range(3):
        for g, g_ref in enumerate((g0_ref, g1_ref, g2_ref)):
            dil = g_ref.shape[0]
            for r in range(dil):
                rows = pl.ds(r, tm // dil, stride=dil) if dil > 1 else slice(None)
                for c in range(per):
                    lo = sec * width + c * LANES
                    g_ref[r, :, lo:lo + LANES] = r_scr[sec * n_slab + g * per + c, rows, :].astype(BF16)


def _attn_projection(x, sc, sh, w, w_vres, rope_tabs):
    bsz, s, d = x.shape
    n = w.shape[1]
    tm = 512
    has_vres = w_vres is not None
    tab = pl.BlockSpec((None, tm, LANES), lambda b, i: (b, i, 0))
    in_specs = [pl.BlockSpec((None, tm, d), lambda b, i: (b, i, 0)),
                pl.BlockSpec((None, 1, d), lambda b, i: (b, 0, 0)),
                pl.BlockSpec((None, 1, d), lambda b, i: (b, 0, 0)),
                pl.BlockSpec((d, n), lambda b, i: (0, 0)),
                tab, tab, tab]
    out_specs, out_shape = [], []
    for _, dil in ATTN_GROUPS:
        out_specs.append(pl.BlockSpec((None, dil, tm // dil, D_ATTN), lambda b, i: (b, 0, i, 0)))
        out_shape.append(jax.ShapeDtypeStruct((bsz, dil, s // dil, D_ATTN), BF16))
    args = [x, sc, sh, w, *rope_tabs]
    if has_vres:
        in_specs.append(pl.BlockSpec((d, LANES), lambda b, i: (0, 0)))
        out_specs.append(pl.BlockSpec((None, tm, LANES), lambda b, i: (b, i, 0)))
        out_shape.append(jax.ShapeDtypeStruct((bsz, s, LANES), F32))
        args.append(w_vres)
    outs = pl.pallas_call(
        functools.partial(_attn_proj_kernel, has_vres=has_vres),
        grid=(bsz, s // tm),
        in_specs=in_specs,
        out_specs=out_specs,
        out_shape=out_shape,
        scratch_shapes=[pltpu.VMEM((n // LANES, tm, LANES), F32)],
        compiler_params=_cparams(("arbitrary", "arbitrary")),
        name="attn_projection",
    )(*args)
    return outs[0:3], (outs[3] if has_vres else None)


def _rglru_kernel(xa_ref, ga_ref, cw_ref, vec_ref, wa_ref, wx_ref, y_ref, xbuf, hc):
    t = pl.program_id(2)
    tt, cwid = xa_ref.shape

    @pl.when(t == 0)
    def _():
        xbuf[0:SUBLANES, :] = jnp.zeros((SUBLANES, cwid), F32)
        hc[...] = jnp.zeros_like(hc)

    xa = xa_ref[...]
    xbuf[SUBLANES:SUBLANES + tt, :] = xa
    cw = cw_ref[...]
    vec = vec_ref[...]
    xc = vec[0:1] + cw[0:1] * xa
    for j in range(1, CONV_A):
        xc = xc + cw[j:j + 1] * xbuf[SUBLANES - j:SUBLANES - j + tt, :]
    xbuf[0:SUBLANES, :] = xbuf[tt:tt + SUBLANES, :]

    xb = xc.astype(BF16)
    r = _sigmoid(jnp.dot(xb, wa_ref[...], preferred_element_type=F32) + vec[1:2])
    i = _sigmoid(jnp.dot(xb, wx_ref[...], preferred_element_type=F32) + vec[2:3])
    log_a = (-LRU_C) * r * _softplus(-vec[3:4])
    a = jnp.exp(log_a)
    om = -jnp.tanh(log_a) * (a * a + 1.0)
    b = jnp.where(om > 0.0, om * lax.rsqrt(om), 0.0) * (i * xc)

    ng = tt // SUBLANES
    a = a.reshape(ng, SUBLANES, cwid)
    b = b.reshape(ng, SUBLANES, cwid)
    rows = lax.broadcasted_iota(jnp.int32, (ng, SUBLANES, cwid), 1)
    step = 1
    while step < SUBLANES:
        a_sh = pltpu.roll(a, step, 1)
        b_sh = pltpu.roll(b, step, 1)
        m = rows >= step
        b = jnp.where(m, a * b_sh + b, b)
        a = jnp.where(m, a * a_sh, a)
        step *= 2
    g = ga_ref[...]
    gelu = 0.5 * g * (1.0 + jnp.tanh(0.7978845608028654 * (g + 0.044715 * (g * g * g))))
    carry = hc[0:1, :]
    hs = []
    for i in range(ng):
        h = a[i] * carry + b[i]
        carry = h[SUBLANES - 1:SUBLANES, :]
        hs.append(h)
    hc[0:1, :] = carry
    y_ref[...] = (jnp.concatenate(hs, axis=0) * gelu).astype(BF16)


def _rglru(z, conv_w, vec, wa_bd, wx_bd):
    bsz, s, _ = z.shape
    tt, cwid = 1024, 256
    nc = D_RNN // cwid
    return pl.pallas_call(
        _rglru_kernel,
        grid=(bsz, nc, s // tt),
        in_specs=[pl.BlockSpec((None, tt, cwid), lambda b, c, t: (b, t, OFF_XA // cwid + c)),
                  pl.BlockSpec((None, tt, cwid), lambda b, c, t: (b, t, OFF_GA // cwid + c)),
                  pl.BlockSpec((CONV_A, cwid), lambda b, c, t: (0, c)),
                  pl.BlockSpec((SUBLANES, cwid), lambda b, c, t: (0, c)),
                  pl.BlockSpec((None, cwid, cwid), lambda b, c, t: (c, 0, 0)),
                  pl.BlockSpec((None, cwid, cwid), lambda b, c, t: (c, 0, 0))],
        out_specs=pl.BlockSpec((None, tt, cwid), lambda b, c, t: (b, t, c)),
        out_shape=jax.ShapeDtypeStruct((bsz, s, D_RNN), BF16),
        scratch_shapes=[pltpu.VMEM((tt + SUBLANES, cwid), F32), pltpu.VMEM((SUBLANES, cwid), F32)],
        compiler_params=_cparams(("arbitrary", "arbitrary", "arbitrary")),
        name="rglru",
    )(z, z, conv_w, vec, wa_bd, wx_bd)


def _rope_table_kernel(pos_ref, freq_ref, cos_ref, sa_ref, sb_ref):
    tt = pos_ref.shape[0]
    ang = pos_ref[...].astype(F32) * freq_ref[...]
    lane = lax.broadcasted_iota(jnp.int32, (tt, LANES), 1) % HEAD_DIM
    sn = jnp.sin(ang)
    half = ROT_DIM // 2
    cos_ref[...] = jnp.cos(ang)
    sa_ref[...] = jnp.where((lane >= half) & (lane < ROT_DIM), sn, 0.0)
    sb_ref[...] = jnp.where(lane < half, -sn, 0.0)


def _rope_tables(pos3, freq_lane):
    bsz, s, _ = pos3.shape
    tt = 1024
    out = pl.BlockSpec((None, tt, LANES), lambda b, t: (b, t, 0))
    return pl.pallas_call(
        _rope_table_kernel,
        grid=(bsz, s // tt),
        in_specs=[pl.BlockSpec((None, tt, 1), lambda b, t: (b, t, 0)),
                  pl.BlockSpec((1, LANES), lambda b, t: (0, 0))],
        out_specs=[out] * 3,
        out_shape=[jax.ShapeDtypeStruct((bsz, s, LANES), F32)] * 3,
        compiler_params=_cparams(("arbitrary", "arbitrary")),
        name="rope_tables",
    )(pos3, freq_lane)


def _attn_kernel(q_ref, kp_ref, kc_ref, vp_ref, vc_ref, o_ref, lse_ref):
    nq = ATTN_BLOCK
    n_res, n_rows, width = q_ref.shape
    nh = HEADS_PER_GROUP
    nk = 2 * nq
    head = lax.broadcasted_iota(jnp.int32, (nq, width), 1) // HEAD_DIM
    qi = lax.broadcasted_iota(jnp.int32, (nh * nq, nk), 0) % nq
    kj = lax.broadcasted_iota(jnp.int32, (nh * nq, nk), 1)
    band = (kj >= qi) & (kj <= qi + nq)
    for rr in range(n_res):
        k_all = jnp.concatenate([kp_ref[rr], kc_ref[rr]], axis=0)
        v_all = jnp.concatenate([vp_ref[rr], vc_ref[rr]], axis=0)
        for j in range(n_rows // nq):
            q = q_ref[rr, j * nq:(j + 1) * nq, :]
            k = k_all[j * nq:j * nq + nk]
            v = v_all[j * nq:j * nq + nk]
            qcat = jnp.concatenate([jnp.where(head == h, q, jnp.zeros_like(q))
                                    for h in range(nh)], axis=0)
            s = lax.dot_general(qcat, k, (((1,), (1,)), ((), ())), preferred_element_type=F32)
            if j == 0:
                valid = band & ((pl.program_id(2) > 0) | (kj >= nq))
            else:
                valid = band
            s = jnp.where(valid, s, -1e30)
            m = jnp.max(s, axis=1, keepdims=True)
            p = jnp.exp(s - m)
            l = jnp.sum(p, axis=1, keepdims=True)
            pv = jnp.dot(p.astype(BF16), v, preferred_element_type=F32) / l
            lse_rows = m + jnp.log(l)
            o = jnp.zeros((nq, width), F32)
            lse = jnp.zeros((nq, width), F32)
            for h in range(nh):
                hm = head == h
                rows = slice(h * nq, (h + 1) * nq)
                o = jnp.where(hm, pv[rows], o)
                lse = jnp.where(hm, lse_rows[rows], lse)
            o_ref[rr, j * nq:(j + 1) * nq, :] = o
            lse_ref[rr, j * nq:(j + 1) * nq, :] = lse


def _attention_group(qkv, g):
    bsz, dil, sub, _ = qkv.shape
    width = HEADS_PER_GROUP * HEAD_DIM
    nb = ATTN_BLOCK

    per = min(ATTN_PER_STEP, sub // nb)
    n_res = min(dil, ATTN_PER_STEP // per)

    def cur(sec):
        return lambda b, r, m: (b, r, m, sec)

    def prev(sec):
        return lambda b, r, m: (b, r, jnp.maximum(m * per - 1, 0), sec)

    blk = (None, n_res, per * nb, width)
    pblk = (None, n_res, nb, width)
    return pl.pallas_call(
        _attn_kernel,
        grid=(bsz, dil // n_res, sub // (per * nb)),
        in_specs=[pl.BlockSpec(blk, cur(0)),
                  pl.BlockSpec(pblk, prev(1)), pl.BlockSpec(blk, cur(1)),
                  pl.BlockSpec(pblk, prev(2)), pl.BlockSpec(blk, cur(2))],
        out_specs=[pl.BlockSpec(blk, cur(0)), pl.BlockSpec(blk, cur(0))],
        out_shape=[jax.ShapeDtypeStruct((bsz, dil, sub, width), F32)] * 2,
        compiler_params=_cparams(("arbitrary", "arbitrary", "arbitrary")),
        name=f"dilated_attention_g{g}",
    )(qkv, qkv, qkv, qkv, qkv)


def _token_shift(z, halo, first, mu):
    ext = jnp.concatenate([jnp.where(first, 0.0, halo), z], axis=0)
    zp = pltpu.roll(ext, 1, 0)[SUBLANES:]
    return z + (zp - z) * mu


def _wkv_tile(r, k, v, a, b, lw, s_scr):
    L = WKV_CHUNK
    n_rows, width = r.shape
    n_chunks = n_rows // L
    n_pairs = width // LANES
    row_in_chunk = lax.broadcasted_iota(jnp.int32, (n_rows, width), 0) % L
    lc = lw
    step = 1
    while step < L:
        lc = lc + jnp.where(row_in_chunk >= step, pltpu.roll(lc, step, 0), 0.0)
        step *= 2
    lc_ends = [lc[(c + 1) * L - 1:(c + 1) * L, :] for c in range(n_chunks)]
    lc_last = jnp.concatenate([jnp.broadcast_to(e, (L, width)) for e in lc_ends], axis=0)
    e_inv = jnp.exp(-lc)
    e_tail = jnp.exp(lc_last - lc)
    a_hat = a * jnp.exp(lc - lw)
    r_hat = r * jnp.exp(lc)
    b_hat = b * e_inv
    k_hat = k * e_inv
    b_til = b * e_tail
    k_til = k * e_tail
    d_last = [jnp.exp(e) for e in lc_ends]

    even = lax.broadcasted_iota(jnp.int32, (L, LANES), 1) < HEAD_DIM

    def cat(x):
        x = x.astype(BF16)
        zero = jnp.zeros_like(x)
        return jnp.concatenate([jnp.where(even, x, zero), jnp.where(even, zero, x)], axis=0)

    def stack(*xs):
        return jnp.concatenate([x.astype(BF16) for x in xs], axis=0)

    ti = lax.broadcasted_iota(jnp.int32, (L, 2 * L), 0)
    tj = lax.broadcasted_iota(jnp.int32, (L, 2 * L), 1) % L
    strict = ti > tj
    incl = ti >= tj
    eye = jnp.where(ti == tj, 1.0, 0.0)
    pi = lax.broadcasted_iota(jnp.int32, (LANES, LANES), 0)
    qi = lax.broadcasted_iota(jnp.int32, (LANES, LANES), 1)
    same_head = (pi // HEAD_DIM) == (qi // HEAD_DIM)

    pairs = range(n_pairs)
    qm, c_mat, y0 = [], [], []
    for c0 in range(0, n_chunks, WKV_GROUP):
        where = [(slice(c * L, (c + 1) * L), slice(p * LANES, (p + 1) * LANES))
                 for c in range(c0, c0 + WKV_GROUP) for p in pairs]
        items = range(len(where))
        v_c = [cat(v[w]) for w in where]
        gram = [_dot_nt(stack(a_hat[w], r_hat[w]),
                        jnp.concatenate([cat(b_hat[w]), cat(k_hat[w])], axis=0))
                for w in where]
        nmat = [jnp.where(strict, gram[i][:L, :LANES], 0.0) for i in items]
        a_ak = [jnp.where(strict, gram[i][:L, LANES:], 0.0) for i in items]
        a_rb = [jnp.where(incl, gram[i][L:, :LANES], 0.0) for i in items]
        a_rk = [jnp.where(incl, gram[i][L:, LANES:], 0.0) for i in items]
        av = [_dot(stack(a_ak[i], a_rk[i]), v_c[i]) for i in items]
        tinv = [eye + nmat[i] for i in items]
        npow = [_dot(nmat[i], cat(nmat[i])) for i in items]
        span = 4
        while span < L:
            prod = [_dot(stack(tinv[i], npow[i]), cat(npow[i])) for i in items]
            tinv = [tinv[i] + prod[i][:L] for i in items]
            npow = [prod[i][L:] for i in items]
            span *= 2
        tinv = [tinv[i] + _dot(tinv[i], cat(npow[i])) for i in items]
        wu = [_dot(tinv[i], jnp.concatenate([cat(a_hat[where[i]]), cat(av[i][:L])], axis=1))
              for i in items]
        qy = [_dot(a_rb[i], jnp.concatenate([cat(wu[i][:, :LANES]), cat(wu[i][:, LANES:])],
                                            axis=1)) for i in items]
        mc = [_dot_tn(stack(b_til[where[i]], k_til[where[i]]),
                      jnp.concatenate([wu[i].astype(BF16),
                                       jnp.concatenate([jnp.zeros((L, LANES), BF16),
                                                        v[where[i]].astype(BF16)], axis=1)],
                                      axis=0))
              for i in items]
        for i in items:
            c = c0 + i // n_pairs
            dl = jnp.broadcast_to(d_last[c][:, where[i][1]], (LANES, LANES))
            m_mat = jnp.where(same_head, mc[i][:, :LANES], 0.0) + jnp.where(pi == qi, dl, 0.0)
            c_mat.append(jnp.where(same_head, mc[i][:, LANES:], 0.0))
            qm.append(stack(r_hat[where[i]] + qy[i][:, :LANES], m_mat))
            y0.append(qy[i][:, LANES:] + av[i][L:])

    state = [s_scr[p] for p in pairs]
    rows = []
    for c in range(n_chunks):
        ys = []
        for p in pairs:
            i = c * n_pairs + p
            out = _dot(qm[i], state[p])
            ys.append(out[:L] + y0[i])
            state[p] = out[L:] + c_mat[i]
        rows.append(jnp.concatenate(ys, axis=1))
    for p in pairs:
        s_scr[p] = state[p]
    return jnp.concatenate(rows, axis=0)


def _rwkv_kernel(*refs, has_vres):
    if has_vres:
        (zr_ref, zk_ref, zv_ref, zl_ref, hr_ref, hk_ref, hv_ref, hl_ref, vec_ref, mul_ref,
         w2_ref, a2_ref, g2_ref, zv1_ref, hv1_ref, vf_ref, muv_ref, v2_ref, y_ref, s_scr) = refs
    else:
        (zr_ref, zk_ref, zv_ref, zl_ref, hr_ref, hk_ref, hv_ref, hl_ref, vec_ref, mul_ref,
         w2_ref, a2_ref, g2_ref, y_ref, vf_out, s_scr) = refs
    first = pl.program_id(1) == 0

    @pl.when(first)
    def _():
        s_scr[...] = jnp.zeros_like(s_scr)

    vec = vec_ref[...]
    r = _token_shift(zr_ref[...], hr_ref[...], first, vec[0:1])
    k = _token_shift(zk_ref[...], hk_ref[...], first, vec[1:2])
    v = _token_shift(zv_ref[...], hv_ref[...], first, vec[2:3])
    lora = _token_shift(zl_ref[...], hl_ref[...], first, mul_ref[...])
    wa_in = lora[:, 0:LANES]
    w_pre = vec[3:4] + _dot(jnp.tanh(wa_in), w2_ref[...])
    lw = (-DECAY_SCALE) * _sigmoid(w_pre)
    alpha = _sigmoid(vec[4:5] + _dot(wa_in, a2_ref[...]))
    g = _dot(_sigmoid(lora[:, LANES:2 * LANES]), g2_ref[...])
    if has_vres:
        zv1 = _token_shift(zv1_ref[...], hv1_ref[...], first, muv_ref[...])
        mix = _sigmoid(vec[7:8] + _dot(zv1, v2_ref[...]))
        v = v + (vf_ref[...] - v) * mix
    else:
        vf_out[...] = v
    ones = _head_ones()
    kk = k * vec[5:6]
    kk = kk * lax.rsqrt(_head_sum(kk * kk, ones) + 1e-12)
    k = k * (1.0 + (alpha - 1.0) * vec[6:7])
    a = -kk
    b = kk * alpha

    y = _wkv_tile(r, k, v, a, b, lw, s_scr)

    inv_n = 1.0 / HEAD_DIM
    mu = _head_sum(y, ones) * inv_n
    dy = y - mu
    var = _head_sum(dy * dy, ones) * inv_n
    gn = dy * lax.rsqrt(var + RWKV_GN_EPS) * vec[9:10] + vec[10:11]
    bonus = _head_sum(r * k * vec[8:9], ones) * v
    y_ref[...] = ((gn + bonus) * g).astype(BF16)


def _rwkv(z, zv1, v_first, vec, mu_l, w2p, a2p, g2, mu_v1, v2p):
    bsz, s, _ = z.shape
    tt = RWKV_TILE
    d = D_RWKV
    has_vres = zv1 is not None
    hstep = tt // SUBLANES

    def col(off, w):
        return lambda b, t: (b, t, off // w)

    def hcol(off, w):
        return lambda b, t: (b, jnp.maximum(t * hstep - 1, 0), off // w)

    lw = 2 * LANES
    in_specs = [pl.BlockSpec((None, tt, d), col(OFF_R, d)),
                pl.BlockSpec((None, tt, d), col(OFF_K, d)),
                pl.BlockSpec((None, tt, d), col(OFF_V, d)),
                pl.BlockSpec((None, tt, lw), col(OFF_LORA, lw)),
                pl.BlockSpec((None, SUBLANES, d), hcol(OFF_R, d)),
                pl.BlockSpec((None, SUBLANES, d), hcol(OFF_K, d)),
                pl.BlockSpec((None, SUBLANES, d), hcol(OFF_V, d)),
                pl.BlockSpec((None, SUBLANES, lw), hcol(OFF_LORA, lw)),
                pl.BlockSpec(vec.shape, lambda b, t: (0, 0)),
                pl.BlockSpec((1, lw), lambda b, t: (0, 0)),
                pl.BlockSpec((LANES, d), lambda b, t: (0, 0)),
                pl.BlockSpec((LANES, d), lambda b, t: (0, 0)),
                pl.BlockSpec((LANES, d), lambda b, t: (0, 0))]
    args = [z, z, z, z, z, z, z, z, vec, mu_l, w2p, a2p, g2]
    row = pl.BlockSpec((None, tt, d), lambda b, t: (b, t, 0))
    out_specs = [row]
    out_shape = [jax.ShapeDtypeStruct((bsz, s, d), BF16)]
    if has_vres:
        in_specs += [pl.BlockSpec((None, tt, LANES), lambda b, t: (b, t, 0)),
                     pl.BlockSpec((None, SUBLANES, LANES),
                                  lambda b, t: (b, jnp.maximum(t * hstep - 1, 0), 0)),
                     row,
                     pl.BlockSpec((1, LANES), lambda b, t: (0, 0)),
                     pl.BlockSpec((LANES, d), lambda b, t: (0, 0))]
        args += [zv1, zv1, v_first, mu_v1, v2p]
    else:
        out_specs.append(row)
        out_shape.append(jax.ShapeDtypeStruct((bsz, s, d), F32))
    outs = pl.pallas_call(
        functools.partial(_rwkv_kernel, has_vres=has_vres),
        grid=(bsz, s // tt),
        in_specs=in_specs,
        out_specs=out_specs,
        out_shape=out_shape,
        scratch_shapes=[pltpu.VMEM((d // LANES, LANES, LANES), F32)],
        compiler_params=_cparams(("arbitrary", "arbitrary")),
        name="rwkv7",
    )(*args)
    return outs[0], (v_first if has_vres else outs[1])


def _interleave(ref, scr):
    dil, n, w = ref.shape
    if dil == 1:
        return ref[0]
    for r in range(dil):
        for c in range(w // LANES):
            scr[c, pl.ds(r, n, stride=dil), :] = ref[r, :, c * LANES:(c + 1) * LANES]
    return jnp.concatenate([scr[c] for c in range(w // LANES)], axis=1)


def _merge_kernel(x_ref, ya_ref, yc_ref, o0_ref, o1_ref, o2_ref, l0_ref, l1_ref, l2_ref,
                  ga_ref, gb_ref, gc_ref, gt_ref, pa_ref, pb_ref, pc_ref, wo_ref, ln_ref, out_ref,
                  so1, so2, sl1, sl2):
    l0 = l0_ref[0]
    l1 = _interleave(l1_ref, sl1)
    l2 = _interleave(l2_ref, sl2)
    o0 = o0_ref[0]
    o1 = _interleave(o1_ref, so1)
    o2 = _interleave(o2_ref, so2)
    lm = jnp.maximum(jnp.maximum(l0, l1), l2)
    e0 = jnp.exp(l0 - lm)
    e1 = jnp.exp(l1 - lm)
    e2 = jnp.exp(l2 - lm)
    yb = ((e0 * o0 + e1 * o1 + e2 * o2) / (e0 + e1 + e2)).astype(BF16)
    ln = ln_ref[...]
    gt = 1.0 + gt_ref[...]
    n_half = 2
    hrows = x_ref.shape[0] // n_half
    for hf in range(n_half):
        rows = slice(hf * hrows, (hf + 1) * hrows)
        merged = (jax.nn.sigmoid(ga_ref[rows, :])
                  * jnp.dot(ya_ref[rows, :], pa_ref[...], preferred_element_type=F32)
                  + jax.nn.sigmoid(gb_ref[rows, :])
                  * jnp.dot(yb[rows], pb_ref[...], preferred_element_type=F32)
                  + jax.nn.sigmoid(gc_ref[rows, :])
                  * jnp.dot(yc_ref[rows, :], pc_ref[...], preferred_element_type=F32))
        y = _dot(merged, wo_ref[...])
        out_ref[rows, :] = _layer_norm(ALPHA * x_ref[rows, :] + gt * y, ln[0:1], ln[1:2])


def _merge(x, z, ya, yc, att, gt, pa, pb, pc, wo, ln):
    bsz, s, d = x.shape
    tm = 512
    wa = D_ATTN_OUT
    row = lambda w, off=0: pl.BlockSpec((None, tm, w), lambda b, i: (b, i, off // w))
    const = lambda shp: pl.BlockSpec(shp, lambda b, i: (0,) * len(shp))
    res = lambda dil: pl.BlockSpec((None, dil, tm // dil, wa), lambda b, i: (b, 0, i, 0))
    (o0, l0), (o1, l1), (o2, l2) = att
    att_specs = [res(dil) for _, dil in ATTN_GROUPS] * 2
    return pl.pallas_call(
        _merge_kernel,
        grid=(bsz, s // tm),
        scratch_shapes=[pltpu.VMEM((wa // LANES, tm, LANES), F32)] * 4,
        in_specs=[row(d), row(d), row(d)] + att_specs
                 + [row(d, OFF_GATE), row(d, OFF_GATE + d), row(d, OFF_GATE + 2 * d),
                    pl.BlockSpec((None, 1, d), lambda b, i: (b, 0, 0)),
                    const((d, d)), const((wa, d)), const((d, d)), const((d, d)),
                    const((SUBLANES, d))],
        out_specs=row(d),
        out_shape=jax.ShapeDtypeStruct((bsz, s, d), F32),
        compiler_params=_cparams(("arbitrary", "arbitrary")),
        name="merge_proj_ln",
    )(x, ya, yc, o0, o1, o2, l0, l1, l2, z, z, z, gt, pa, pb, pc, wo, ln)


def _ffn_kernel(x_ref, halo_ref, sc_ref, sh_ref, gt_ref, wu_ref, cw_ref, wd_ref, ln_ref, out_ref):
    i = pl.program_id(1)
    hr = halo_ref.shape[0]
    sc = 1.0 + sc_ref[...]
    sh = sh_ref[...]
    x = x_ref[...]
    hh = jnp.where(i == 0, 0.0, halo_ref[...] * sc + sh)
    h = jnp.concatenate([hh.astype(BF16), (x * sc + sh).astype(BF16)], axis=0)

    def up(c):
        lo, n = FFN_SLABS[c]
        return [jnp.dot(h, wu_ref[:, off:off + n], preferred_element_type=F32)
                for off in (lo, D_FF + lo)]

    def conv(u, off):
        cw = cw_ref[:, off:off + u.shape[1]]
        out = cw[CONV_F:CONV_F + 1] + cw[0:1] * u[hr:]
        for j in range(1, CONV_F):
            out = out + cw[j:j + 1] * pltpu.roll(u, j, 0)[hr:]
        return out

    acc = None
    u_next = up(0)
    for c, (lo, n) in enumerate(FFN_SLABS):
        ug, uv = u_next
        if c + 1 < len(FFN_SLABS):
            u_next = up(c + 1)
        cg = conv(ug, lo)
        cv = conv(uv, D_FF + lo)
        act = (cg * _sigmoid(cg) * cv).astype(BF16)
        part = jnp.dot(act, wd_ref[lo:lo + n, :], preferred_element_type=F32)
        acc = part if acc is None else acc + part
    ln = ln_ref[...]
    out_ref[...] = _layer_norm(ALPHA * x + (1.0 + gt_ref[...]) * acc, ln[0:1], ln[1:2])


def _conv_ffn(x, sc, sh, gt, w_up, conv_wb, w_down, ln):
    bsz, s, d = x.shape
    tm = 512
    hr = BF16_ROWS
    mod = pl.BlockSpec((None, 1, d), lambda b, i: (b, 0, 0))
    once = pl.Buffered(1)
    return pl.pallas_call(
        _ffn_kernel,
        grid=(bsz, s // tm),
        in_specs=[pl.BlockSpec((None, tm, d), lambda b, i: (b, i, 0)),
                  pl.BlockSpec((None, hr, d), lambda b, i: (b, jnp.maximum(i * (tm // hr) - 1, 0), 0)),
                  mod, mod, mod,
                  pl.BlockSpec(w_up.shape, lambda b, i: (0, 0), pipeline_mode=once),
                  pl.BlockSpec(conv_wb.shape, lambda b, i: (0, 0), pipeline_mode=once),
                  pl.BlockSpec(w_down.shape, lambda b, i: (0, 0), pipeline_mode=once),
                  pl.BlockSpec((SUBLANES, d), lambda b, i: (0, 0))],
        out_specs=pl.BlockSpec((None, tm, d), lambda b, i: (b, i, 0)),
        out_shape=jax.ShapeDtypeStruct((bsz, s, d), F32),
        compiler_params=_cparams(("arbitrary", "arbitrary")),
        name="conv_ffn_ln",
    )(x, x, sc, sh, gt, w_up, conv_wb, w_down, ln)


def _rows(vectors, n_rows=SUBLANES):
    m = jnp.stack([v.astype(F32) for v in vectors], axis=0)
    return jnp.pad(m, ((0, n_rows - m.shape[0]), (0, 0)))


def _block_diag(w, per):
    g, n, _ = w.shape
    w = w.reshape(g // per, per, n, n)
    eye = jnp.eye(per, dtype=w.dtype)
    out = jnp.einsum('cpij,pq->cpiqj', w, eye)
    return out.reshape(g // per, per * n, per * n)


def kernel(x, c, positions, mod_w, mod_b, w_in, w_in_vres, conv_a_w, conv_a_b, lru_wa, lru_ba, lru_wx, lru_bx, lru_lambda, rwkv_mu, mu_vres, w0, w2, a0, a2, g2, v0, v2, k_k, k_a, r_k, ln_x_w, ln_x_b, proj_a, proj_b, proj_c, w_o, ln1_w, ln1_b, ffn_up, ffn_conv_w, ffn_conv_b, ffn_down, ln2_w, ln2_b):
    bsz, s, d = x.shape
    mod = _modulation(c, mod_w, mod_b)
    half = ROT_DIM // 2
    inv_freq = ROPE_THETA ** (-jnp.arange(half, dtype=F32) / half)
    lane = jnp.arange(LANES)
    freq_lane = jnp.where((lane % HEAD_DIM) < ROT_DIM, inv_freq[lane % half], 0.0).reshape(1, LANES)
    rope_tabs = _rope_tables(positions.reshape(bsz, s, 1), freq_lane)
    zero_d = jnp.zeros((d,), F32)

    v_first = None
    for l in range(DEPTH):
        m6 = mod[l].reshape(bsz, 6, 1, d)
        sh1, sc1, gt1, sh2, sc2, gt2 = (m6[:, i] for i in range(6))

        wl = w_in[l]
        w_main = jnp.concatenate([wl[:, 0:2048], wl[:, 4352:7424], wl[:, 7680:10752],
                                  wl[:, 7424:7680]], axis=1).astype(BF16)
        if l == 0:
            w_vres = None
        else:
            w_vres = jnp.pad(w_in_vres[l - 1], ((0, 0), (0, LANES - MV_LORA))).astype(BF16)
        z = _main_projection(x, sc1, sh1, w_main)
        qkv_groups, zv1 = _attn_projection(x, sc1, sh1, wl[:, 2048:4352].astype(BF16), w_vres,
                                           rope_tabs)

        per = 256 // RNN_BLOCK
        y_a = _rglru(z, conv_a_w[l],
                     _rows([conv_a_b[l], lru_ba[l], lru_bx[l], lru_lambda[l]]),
                     _block_diag(lru_wa[l], per).astype(BF16),
                     _block_diag(lru_wx[l], per).astype(BF16))

        att = [_attention_group(qkv_groups[g], g) for g in range(N_GROUPS)]

        mu = rwkv_mu[l]
        vec = _rows([mu[0:1024], mu[1024:2048], mu[2048:3072], w0[l], a0[l], k_k[l], k_a[l],
                     v0[l - 1] if l > 0 else zero_d, r_k[l].reshape(-1), ln_x_w[l], ln_x_b[l]],
                    n_rows=2 * SUBLANES)
        mu_l = mu[3072:3328].reshape(1, 2 * LANES)
        w2p = jnp.pad(w2[l], ((0, LANES - LORA_W), (0, 0))).astype(BF16)
        a2p = jnp.pad(a2[l], ((LORA_W, 0), (0, 0))).astype(BF16)
        if l == 0:
            y_c, v_first = _rwkv(z, None, None, vec, mu_l, w2p, a2p, g2[l].astype(BF16), None, None)
        else:
            mu_v1 = jnp.pad(mu_vres[l - 1], (0, LANES - MV_LORA)).reshape(1, LANES)
            v2p = jnp.pad(v2[l - 1], ((0, LANES - MV_LORA), (0, 0))).astype(BF16)
            y_c, _ = _rwkv(z, zv1, v_first, vec, mu_l, w2p, a2p, g2[l].astype(BF16), mu_v1, v2p)

        x = _merge(x, z, y_a, y_c, att, gt1, proj_a[l].astype(BF16), proj_b[l].astype(BF16),
                   proj_c[l].astype(BF16), w_o[l].astype(BF16), _rows([ln1_w[l], ln1_b[l]]))

        conv_wb = jnp.concatenate([ffn_conv_w[l], ffn_conv_b[l][None, :]], axis=0)
        conv_wb = jnp.pad(conv_wb, ((0, SUBLANES - conv_wb.shape[0]), (0, 0)))
        x = _conv_ffn(x, sc2, sh2, gt2, ffn_up[l].astype(BF16), conv_wb,
                      ffn_down[l].astype(BF16), _rows([ln2_w[l], ln2_b[l]]))
    return x
```
